```python
import math
import jax, jax.numpy as jnp
from jax import lax
import numpy as np

D_MODEL = 1024
BATCH = 8
SEQ = 2048
DEPTH = 1
DEC_BATCH = 128
DEC_SEQ = 8
PAST_LEN = 16384
PAGE_SIZE = 128

D_MIX = 2 * D_MODEL
D_M = D_MIX // 2
D_H = D_MIX - D_M
M_HEADS = 4
M_DV = D_M // M_HEADS
M_DK = M_DV // 2
CONV_W = 4
H_HEADS = 8
H_DK = D_H // H_HEADS
H_DV = D_H // H_HEADS
CHUNK = 64
EPS = 1e-6
SPLITS = (D_M, D_M, D_M, D_H, D_H, D_H, D_H)
N_IN = sum(SPLITS)
GATE_IN = 2 * M_HEADS * M_DK + D_M

kernel_name = "hybrid_mlstm_hgrn2_step"


def rmsnorm(x, g):
    xf = x.astype(jnp.float32)
    y = xf * lax.rsqrt(jnp.mean(xf * xf, -1, keepdims=True) + EPS)
    return y * g.astype(jnp.float32)


def layernorm(x, g):
    xf = x.astype(jnp.float32)
    mu = jnp.mean(xf, -1, keepdims=True)
    var = jnp.mean(jnp.square(xf - mu), -1, keepdims=True)
    return (xf - mu) * lax.rsqrt(var + EPS) * g.astype(jnp.float32)


def causal_conv(x, buf, w, b):
    T = x.shape[1]
    xp = jnp.concatenate([buf.astype(x.dtype), x], axis=1)
    y = sum(xp[:, j:j + T] * w[j] for j in range(CONV_W)) + b
    return y, xp[:, -(CONV_W - 1):]


def to_chunks(a, nC, L):
    return jnp.moveaxis(a.reshape(a.shape[:2] + (nC, L) + a.shape[3:]), 2, 0)


def from_chunks(a):
    a = jnp.moveaxis(a, 0, 2)
    return a.reshape(a.shape[:2] + (a.shape[2] * a.shape[3],) + a.shape[4:])


def mlstm_scan(q, k, v, ig, lf, C0, n0, m0):
    T = q.shape[2]
    L = math.gcd(T, CHUNK)
    nC = T // L
    causal = jnp.tril(jnp.ones((L, L), bool))

    def step(carry, inp):
        C, n, m = carry
        qc, kc, vc, ic, fc = inp
        b = jnp.cumsum(fc, -1)
        logD = jnp.where(causal, b[..., :, None] - b[..., None, :] + ic[..., None, :], -jnp.inf)
        m_inter = b + m[..., None]
        m_t = jnp.maximum(jnp.max(logD, -1), m_inter)
        D = jnp.exp(logD - m_t[..., None])
        g = jnp.exp(m_inter - m_t)
        s = jnp.einsum('bhtk,bhsk->bhts', qc, kc) * D
        num = jnp.einsum('bhts,bhsv->bhtv', s, vc) + g[..., None] * jnp.einsum('bhtk,bhkv->bhtv', qc, C)
        den = jnp.sum(s, -1) + g * jnp.einsum('bhtk,bhk->bht', qc, n)
        h = num / jnp.maximum(jnp.abs(den), jnp.exp(-m_t))[..., None]
        w = D[..., -1, :]
        decay = g[..., -1]
        C_new = decay[..., None, None] * C + jnp.einsum('bhs,bhsk,bhsv->bhkv', w, kc, vc)
        n_new = decay[..., None] * n + jnp.einsum('bhs,bhsk->bhk', w, kc)
        return (C_new, n_new, m_t[..., -1]), h

    xs = (to_chunks(q, nC, L), to_chunks(k, nC, L), to_chunks(v, nC, L),
          to_chunks(ig, nC, L), to_chunks(lf, nC, L))
    (C, n, m), h = lax.scan(step, (C0, n0, m0), xs)
    return from_chunks(h), C, n, m


def hgrn2_scan(q, k, v, lf, S0):
    T = q.shape[2]
    L = math.gcd(T, CHUNK)
    nC = T // L
    causal = jnp.tril(jnp.ones((L, L), bool))[:, :, None]

    def step(S, inp):
        qc, kc, vc, fc = inp
        b = jnp.cumsum(fc, 2)
        diff = b[:, :, :, None, :] - b[:, :, None, :, :]
        D = jnp.exp(jnp.where(causal, diff, -jnp.inf))
        A = jnp.einsum('bhtk,bhsk,bhtsk->bhts', qc, kc, D)
        o = jnp.einsum('bhts,bhsv->bhtv', A, vc) + jnp.einsum('bhtk,bhkv->bhtv', qc * jnp.exp(b), S)
        kd = kc * jnp.exp(b[:, :, -1:] - b)
        S_new = jnp.exp(b[:, :, -1])[..., None] * S + jnp.einsum('bhsk,bhsv->bhkv', kd, vc)
        return S_new, o

    xs = (to_chunks(q, nC, L), to_chunks(k, nC, L), to_chunks(v, nC, L), to_chunks(lf, nC, L))
    S, o = lax.scan(step, S0, xs)
    return from_chunks(o), S


def mixer_layer(x, conv_buf, C0, n0, m0, S0, lb, g_norm, w_in, conv_w, conv_b, w_q, w_k, w_v,
                w_gate, b_gate, m_ln, m_skip, h_norm, w_out):
    B, T, _ = x.shape
    dt = x.dtype
    f32 = jnp.float32
    xn = rmsnorm(x, g_norm).astype(dt)
    proj = xn @ w_in
    offs = np.cumsum((0,) + SPLITS)
    xm, zm, om, fh, qh, ih, zh = [proj[..., offs[i]:offs[i + 1]] for i in range(len(SPLITS))]

    xc, conv_new = causal_conv(xm, conv_buf, conv_w, conv_b)
    xc = jax.nn.silu(xc)
    xc_h = xc.reshape(B, T, M_HEADS, M_DV)
    xm_h = xm.reshape(B, T, M_HEADS, M_DV)
    q = jnp.einsum('bthd,hdk->bthk', xc_h, w_q)
    k = jnp.einsum('bthd,hdk->bthk', xc_h, w_k)
    v = jnp.einsum('bthd,hdv->bthv', xm_h, w_v)
    gate_in = jnp.concatenate([q.reshape(B, T, -1), k.reshape(B, T, -1), v.reshape(B, T, -1)], -1)
    gates = (gate_in @ w_gate + b_gate).astype(f32)
    ig = jnp.transpose(gates[..., :M_HEADS], (0, 2, 1))
    lf = jnp.transpose(jax.nn.log_sigmoid(gates[..., M_HEADS:]), (0, 2, 1))
    qf = jnp.transpose(q, (0, 2, 1, 3)).astype(f32) * (M_DK ** -0.5)
    kf = jnp.transpose(k, (0, 2, 1, 3)).astype(f32)
    vf = jnp.transpose(v, (0, 2, 1, 3)).astype(f32)
    hm, C1, n1, m1 = mlstm_scan(qf, kf, vf, ig, lf, C0.astype(f32), n0.astype(f32), m0.astype(f32))
    hm = layernorm(jnp.transpose(hm, (0, 2, 1, 3)), m_ln).reshape(B, T, D_M)
    hm = hm * jax.nn.sigmoid(om.astype(f32))
    hm = (hm + m_skip * xc.astype(f32)) * jax.nn.silu(zm.astype(f32))

    fl = fh.astype(f32)
    lbf = lb.astype(f32)
    logf = jnp.log(lbf + (1.0 - lbf) * jax.nn.sigmoid(fl))
    kh = (1.0 - lbf) * jax.nn.sigmoid(-fl)
    qs = jax.nn.silu(qh.astype(f32))
    heads = lambda a: jnp.transpose(a.reshape(B, T, H_HEADS, -1), (0, 2, 1, 3))
    oh, S1 = hgrn2_scan(heads(qs), heads(kh), heads(ih.astype(f32)), heads(logf), S0.astype(f32))
    oh = rmsnorm(jnp.transpose(oh, (0, 2, 1, 3)), h_norm).reshape(B, T, D_H)
    oh = oh * jax.nn.silu(zh.astype(f32))

    mix = jnp.concatenate([hm, oh], -1).astype(dt)
    y = x + mix @ w_out
    return y, conv_new, C1, n1, m1, S1


def setup_inputs(seed: int = 0) -> dict:
    key = jax.random.key(seed)
    ks = jax.random.split(key, 24)
    nrm = jax.random.normal
    f32 = jnp.float32
    b_gate = jnp.concatenate([0.1 * nrm(ks[0], (DEPTH, M_HEADS), f32),
                              3.0 + 0.1 * nrm(ks[1], (DEPTH, M_HEADS), f32)], -1)
    return {
        "x_prompt": nrm(ks[2], (BATCH, SEQ, D_MODEL), f32),
        "x_sample": nrm(ks[3], (DEC_BATCH, DEC_SEQ, D_MODEL), f32),
        "state_mlstm_conv": nrm(ks[4], (DEPTH, DEC_BATCH, CONV_W - 1, D_M), f32),
        "state_mlstm_C": nrm(ks[5], (DEPTH, DEC_BATCH, M_HEADS, M_DK, M_DV), f32),
        "state_mlstm_n": nrm(ks[6], (DEPTH, DEC_BATCH, M_HEADS, M_DK), f32),
        "state_mlstm_m": nrm(ks[7], (DEPTH, DEC_BATCH, M_HEADS), f32),
        "state_hgrn_S": nrm(ks[8], (DEPTH, DEC_BATCH, H_HEADS, H_DK, H_DV), f32),
        "g_norm": 1.0 + 0.02 * nrm(ks[9], (DEPTH, D_MODEL), f32),
        "w_in": nrm(ks[10], (DEPTH, D_MODEL, N_IN), f32) * D_MODEL ** -0.5,
        "conv_w": nrm(ks[11], (DEPTH, CONV_W, D_M), f32) * CONV_W ** -0.5,
        "conv_b": 0.02 * nrm(ks[12], (DEPTH, D_M), f32),
        "w_q": nrm(ks[13], (DEPTH, M_HEADS, M_DV, M_DK), f32) * M_DV ** -0.5,
        "w_k": nrm(ks[14], (DEPTH, M_HEADS, M_DV, M_DK), f32) * M_DV ** -0.5,
        "w_v": nrm(ks[15], (DEPTH, M_HEADS, M_DV, M_DV), f32) * M_DV ** -0.5,
        "w_gate": nrm(ks[16], (DEPTH, GATE_IN, 2 * M_HEADS), f32) * GATE_IN ** -0.5,
        "b_gate": b_gate,
        "m_ln": 1.0 + 0.02 * nrm(ks[17], (DEPTH, M_HEADS, M_DV), f32),
        "m_skip": 1.0 + 0.02 * nrm(ks[18], (DEPTH, D_M), f32),
        "lb_param": 0.5 * nrm(ks[19], (DEPTH + 1, D_H), f32),
        "h_norm": 1.0 + 0.02 * nrm(ks[20], (DEPTH, H_HEADS, H_DV), f32),
        "w_out": nrm(ks[21], (DEPTH, D_MIX, D_MODEL), f32) * D_MIX ** -0.5,
        "g_final": 1.0 + 0.02 * nrm(ks[22], (D_MODEL,), f32),
    }


def reference(x_prompt, x_sample, state_mlstm_conv, state_mlstm_C, state_mlstm_n, state_mlstm_m,
              state_hgrn_S, g_norm, w_in, conv_w, conv_b, w_q, w_k, w_v, w_gate, b_gate, m_ln,
              m_skip, lb_param, h_norm, w_out, g_final):
    f32 = jnp.float32
    lb_all = jnp.cumsum(jax.nn.softmax(lb_param.astype(f32), axis=0), axis=0)
    Bp = x_prompt.shape[0]
    zC = jnp.zeros((Bp, M_HEADS, M_DK, M_DV), f32)
    zn = jnp.zeros((Bp, M_HEADS, M_DK), f32)
    zm = jnp.zeros((Bp, M_HEADS), f32)
    zS = jnp.zeros((Bp, H_HEADS, H_DK, H_DV), f32)
    zconv = jnp.zeros((Bp, CONV_W - 1, D_M), x_prompt.dtype)
    hp, hs = x_prompt, x_sample
    p_new = [[] for _ in range(5)]
    s_new = [[] for _ in range(5)]
    for l in range(DEPTH):
        w = (lb_all[l], g_norm[l], w_in[l], conv_w[l], conv_b[l], w_q[l], w_k[l], w_v[l],
             w_gate[l], b_gate[l], m_ln[l], m_skip[l], h_norm[l], w_out[l])
        hp, *pst = mixer_layer(hp, zconv, zC, zn, zm, zS, *w)
        hs, *sst = mixer_layer(hs, state_mlstm_conv[l], state_mlstm_C[l], state_mlstm_n[l],
                               state_mlstm_m[l], state_hgrn_S[l], *w)
        for i in range(5):
            p_new[i].append(pst[i])
            s_new[i].append(sst[i])
    y_prompt = rmsnorm(hp, g_final).astype(x_prompt.dtype)
    y_sample = rmsnorm(hs, g_final).astype(x_sample.dtype)
    p_conv, p_C, p_n, p_m, p_S = [jnp.stack(a, 0) for a in p_new]
    s_conv, s_C, s_n, s_m, s_S = [jnp.stack(a, 0) for a in s_new]
    return (y_prompt, y_sample, p_conv, p_C, p_n, p_m, p_S, s_conv, s_C, s_n, s_m, s_S)
```

```python
import functools

import jax
import jax.numpy as jnp
from jax import lax
from jax.experimental import pallas as pl
from jax.experimental.pallas import tpu as pltpu

F32 = jnp.float32
BF16 = jnp.bfloat16

EPS = 1e-6
CONV_W = 4
M_HEADS = 4
H_HEADS = 8
N_SPLITS = 7
LANES = 128
SUBLANES = 8
HIST = SUBLANES
PROMPT_CHUNK = 64
HGRN_SUBCHUNK = 32
VMEM_LIMIT_BYTES = 56 * 1024 * 1024


def _sigmoid(x):
    return 1.0 / (1.0 + jnp.exp(-x))


def _silu(x):
    return x * _sigmoid(x)


def _log_sigmoid(x):
    return jnp.minimum(x, 0.0) - jnp.log(1.0 + jnp.exp(-jnp.abs(x)))


def _dot(a, b):
    return jnp.dot(a, b, preferred_element_type=F32)


def _dot_nt(a, b):
    return lax.dot_general(a, b, (((1,), (1,)), ((), ())), preferred_element_type=F32)


def _dot_tn(a, b):
    return lax.dot_general(a, b, (((0,), (0,)), ((), ())), preferred_element_type=F32)


def _transpose_rows(a):
    r = a.shape[0]
    if r < LANES:
        a = jnp.concatenate([a, jnp.zeros((LANES - r, LANES), a.dtype)], axis=0)
    return a.T


def _mixer_kernel(nb, tc, lh, has_init, *refs):
    if has_init:
        (x_ref, conv0_ref, c0_ref, n0_ref, m0_ref, s0_ref), refs = refs[:6], refs[6:]
    else:
        x_ref, refs = refs[0], refs[1:]
    (g_norm_ref, w_in_ref, conv_w_ref, conv_b_ref, w_q_ref, w_k_ref, w_v_ref, w_gate_ref,
     b_gate_ref, m_ln_ref, m_skip_ref, lb_param_ref, h_norm_ref, w_out_ref, g_final_ref,
     y_ref, conv_o_ref, c_o_ref, n_o_ref, m_o_ref, s_o_ref,
     convbuf, xn_s, xc_s, q_s, k_s, v_s, g_s, hm_s, logf_s, kh_s, qs_s, ih_s, oh_s) = refs

    m_rows = nb * tc
    d_model = x_ref.shape[-1]
    d_m = convbuf.shape[-1]
    dk = w_q_ref.shape[-1]
    dv = w_v_ref.shape[-1]
    hdk = d_m // H_HEADS
    q_scale = dk ** -0.5

    if has_init:
        c_i_ref, n_i_ref, m_i_ref, s_i_ref = c0_ref, n0_ref, m0_ref, s0_ref
        convbuf[:, HIST - (CONV_W - 1):HIST, :] = conv0_ref[...]
    else:
        c_i_ref, n_i_ref, m_i_ref, s_i_ref = c_o_ref, n_o_ref, m_o_ref, s_o_ref
        t_idx = pl.program_id(1)

        @pl.when(t_idx == 0)
        def _():
            c_o_ref[...] = jnp.zeros_like(c_o_ref)
            n_o_ref[...] = jnp.zeros_like(n_o_ref)
            m_o_ref[...] = jnp.zeros_like(m_o_ref)
            s_o_ref[...] = jnp.zeros_like(s_o_ref)
            convbuf[:, 0:HIST, :] = jnp.zeros((nb, HIST, d_m), F32)

        @pl.when(t_idx > 0)
        def _():
            convbuf[:, 0:HIST, :] = convbuf[:, tc:tc + HIST, :]

    x = x_ref[...].reshape(m_rows, d_model)
    xn = (x * lax.rsqrt(jnp.mean(x * x, -1, keepdims=True) + EPS) * g_norm_ref[...]).astype(BF16)
    xn_s[...] = xn

    def proj(i):
        return _dot(xn, w_in_ref[:, i * d_m:(i + 1) * d_m])

    xm = proj(0)
    convbuf[:, HIST:HIST + tc, :] = xm.reshape(nb, tc, d_m)
    acc = conv_b_ref[...].reshape(1, 1, d_m)
    for j in range(CONV_W):
        lo = HIST - (CONV_W - 1) + j
        acc = acc + convbuf[:, lo:lo + tc, :] * conv_w_ref[j:j + 1, :].reshape(1, 1, d_m)
    conv_o_ref[...] = convbuf[:, tc + HIST - (CONV_W - 1):tc + HIST, :]
    xc = _silu(acc).reshape(m_rows, d_m)
    xc_s[...] = xc

    xm_b = xm.astype(BF16)
    xc_b = xc.astype(BF16)
    for h in range(M_HEADS):
        xc_h = xc_b[:, h * dv:(h + 1) * dv]
        q_s[:, h * dk:(h + 1) * dk] = _dot(xc_h, w_q_ref[h])
        k_s[:, h * dk:(h + 1) * dk] = _dot(xc_h, w_k_ref[h])
        v_s[:, h * dv:(h + 1) * dv] = _dot(xm_b[:, h * dv:(h + 1) * dv], w_v_ref[h]).astype(v_s.dtype)
    n_qk = M_HEADS * dk
    g_s[...] = (b_gate_ref[...]
                + _dot(q_s[...].astype(BF16), w_gate_ref[0:n_qk, :])
                + _dot(k_s[...].astype(BF16), w_gate_ref[n_qk:2 * n_qk, :])
                + _dot(v_s[...].astype(BF16), w_gate_ref[2 * n_qk:, :]))

    lbp = lb_param_ref[...]
    lbe = jnp.exp(lbp - jnp.max(lbp, axis=0, keepdims=True))
    lb = lbe[0:1, :] / jnp.sum(lbe, axis=0, keepdims=True)
    sg = _sigmoid(proj(3))
    logf_s[...] = jnp.log(lb + (1.0 - lb) * sg)
    kh_s[...] = (1.0 - lb) * (1.0 - sg)
    qs_s[...] = _silu(proj(4))
    ih_s[...] = proj(5).astype(ih_s.dtype)

    row_i = lax.broadcasted_iota(jnp.int32, (tc, tc), 0)
    col_i = lax.broadcasted_iota(jnp.int32, (tc, tc), 1)
    causal = col_i <= row_i
    causal_t = row_i <= col_i
    row_h = lax.broadcasted_iota(jnp.int32, (lh, lh), 0)
    col_h = lax.broadcasted_iota(jnp.int32, (lh, lh), 1)
    causal_h = col_h <= row_h
    tril_h = causal_h.astype(F32).astype(BF16)
    neg_inf = F32(-jnp.inf)

    def seq_body(b, carry):
        r0 = pl.multiple_of(b * tc, SUBLANES)
        rows = pl.ds(r0, tc)

        gb = g_s[rows, :]
        gt = _transpose_rows(gb)
        lf_cols = _log_sigmoid(gb)
        lf_rows = _log_sigmoid(gt[M_HEADS:2 * M_HEADS, 0:tc])
        for h in range(M_HEADS):
            i_row = gt[h:h + 1, 0:tc]
            i_col = gb[:, h:h + 1]
            lf_row = lf_rows[h:h + 1, :]
            lf_col = lf_cols[:, M_HEADS + h:M_HEADS + h + 1]
            b_col = jnp.sum(jnp.where(causal, lf_row, 0.0), axis=1, keepdims=True)
            b_row = jnp.sum(jnp.where(causal_t, lf_col, 0.0), axis=0, keepdims=True)
            m_prev = m_i_ref[b, h:h + 1, 0:1]
            a_row = i_row - b_row
            a_col = i_col - b_col
            log_d = jnp.where(causal, a_row, neg_inf)
            m_col = jnp.maximum(jnp.max(log_d, axis=1, keepdims=True), m_prev)
            d_mat = jnp.exp(log_d - m_col)
            g_col = jnp.exp(m_prev - m_col)

            q = q_s[rows, h * dk:(h + 1) * dk] * q_scale
            k = k_s[rows, h * dk:(h + 1) * dk]
            v_b = v_s[rows, h * dv:(h + 1) * dv].astype(BF16)
            q_b = q.astype(BF16)
            c_prev = c_i_ref[b, h]
            n_prev = n_i_ref[b, h:h + 1, :]

            s_mat = _dot_nt(q_b, k.astype(BF16)) * d_mat
            num = _dot(s_mat.astype(BF16), v_b) + g_col * _dot(q_b, c_prev.astype(BF16))
            den = (jnp.sum(s_mat, axis=1, keepdims=True)
                   + g_col * jnp.sum(q * n_prev, axis=1, keepdims=True))
            m_t = b_col + m_col
            hval = num * (1.0 / jnp.maximum(jnp.abs(den), jnp.exp(-m_t)))

            m_last = m_col[tc - 1:tc, :]
            decay = jnp.exp(m_prev - m_last)
            kw = k * jnp.exp(a_col - m_last)
            c_o_ref[b, h] = decay * c_prev + _dot_tn(kw.astype(BF16), v_b)
            n_o_ref[b, h:h + 1, :] = decay * n_prev + jnp.sum(kw, axis=0, keepdims=True)
            m_o_ref[b, h:h + 1, :] = jnp.broadcast_to(b_col[tc - 1:tc, :] + m_last, (1, LANES))

            mu = jnp.mean(hval, axis=1, keepdims=True)
            hc = hval - mu
            var = jnp.mean(hc * hc, axis=1, keepdims=True)
            hm_s[rows, h * dv:(h + 1) * dv] = (hc * lax.rsqrt(var + EPS)
                                               * m_ln_ref[:, h * dv:(h + 1) * dv])

        for j in range(tc // lh):
            rows_j = pl.ds(pl.multiple_of(r0 + j * lh, SUBLANES), lh)
            lf = logf_s[rows_j, :]
            lf_hi = lf.astype(BF16)
            lf_lo = (lf - lf_hi.astype(F32)).astype(BF16)
            bc = _dot(tril_h, lf_hi) + _dot(tril_h, lf_lo)
            b_mid = bc[lh // 2 - 1:lh // 2, :]
            b_last = bc[lh - 1:lh, :]
            qs = qs_s[rows_j, :]
            kh = kh_s[rows_j, :]
            p_all = (qs * jnp.exp(bc - b_mid)).astype(BF16)
            q_all = (kh * jnp.exp(b_mid - bc)).astype(BF16)
            qe_all = (qs * jnp.exp(bc)).astype(BF16)
            kd_all = (kh * jnp.exp(b_last - bc)).astype(BF16)
            e8 = jnp.concatenate([b_last[:, h * hdk:(h + 1) * hdk] for h in range(H_HEADS)], axis=0)
            dec_cols = jnp.exp(_transpose_rows(e8))
            for h in range(H_HEADS):
                sl = slice(h * hdk, (h + 1) * hdk)
                v_b = ih_s[rows_j, sl].astype(BF16)
                s_prev = s_i_ref[b, h]
                a_mat = jnp.where(causal_h, _dot_nt(p_all[:, sl], q_all[:, sl]), 0.0)
                o = _dot(a_mat.astype(BF16), v_b) + _dot(qe_all[:, sl], s_prev.astype(BF16))
                s_o_ref[b, h] = dec_cols[:, h:h + 1] * s_prev + _dot_tn(kd_all[:, sl], v_b)
                oh_s[rows_j, sl] = (o * lax.rsqrt(jnp.mean(o * o, axis=1, keepdims=True) + EPS)
                                    * h_norm_ref[:, sl])
        return carry

    lax.fori_loop(0, nb, seq_body, 0)

    xn = xn_s[...]
    hm = hm_s[...] * _sigmoid(proj(2))
    hm = (hm + m_skip_ref[...] * xc_s[...]) * _silu(proj(1))
    oh = oh_s[...] * _silu(proj(6))
    y = (x_ref[...].reshape(m_rows, d_model)
         + _dot(hm.astype(BF16), w_out_ref[0:d_m, :])
         + _dot(oh.astype(BF16), w_out_ref[d_m:, :]))
    y = y * lax.rsqrt(jnp.mean(y * y, -1, keepdims=True) + EPS) * g_final_ref[...]
    y_ref[...] = y.reshape(nb, tc, d_model)


def _const_spec(shape):
    zeros = (0,) * len(shape)
    return pl.BlockSpec(shape, lambda *_: zeros, pipeline_mode=pl.Buffered(1))


def _mixer_pass(x, state, weights, nb, tc):
    batch, seq, d_model = x.shape
    (g_norm, w_in, conv_w, conv_b, w_q, w_k, w_v, w_gate, b_gate, m_ln, m_skip, lb_param,
     h_norm, w_out, g_final) = weights
    d_m = conv_w.shape[-1]
    dk, dv = w_q.shape[-1], w_v.shape[-1]
    hdk = d_m // H_HEADS
    has_init = state is not None
    lh = min(tc, HGRN_SUBCHUNK)
    m_rows = nb * tc
    op_dtype = BF16 if tc % (2 * SUBLANES) == 0 else F32

    if has_init:
        assert seq == tc
        grid = (batch // nb,)
        bmap = lambda g: (g, 0, 0)
        bmap4 = lambda g: (g, 0, 0, 0)
        semantics = ("arbitrary",)
    else:
        grid = (batch // nb, seq // tc)
        bmap = lambda g, t: (g, 0, 0)
        bmap4 = lambda g, t: (g, 0, 0, 0)
        semantics = ("arbitrary", "arbitrary")
    x_map = (lambda g: (g, 0, 0)) if has_init else (lambda g, t: (g, t, 0))

    state_specs = [
        pl.BlockSpec((nb, CONV_W - 1, d_m), bmap),
        pl.BlockSpec((nb, M_HEADS, dk, dv), bmap4),
        pl.BlockSpec((nb, M_HEADS, dk), bmap),
        pl.BlockSpec((nb, M_HEADS, LANES), bmap),
        pl.BlockSpec((nb, H_HEADS, hdk, hdk), bmap4),
    ]
    state_shapes = [
        jax.ShapeDtypeStruct((batch, CONV_W - 1, d_m), F32),
        jax.ShapeDtypeStruct((batch, M_HEADS, dk, dv), F32),
        jax.ShapeDtypeStruct((batch, M_HEADS, dk), F32),
        jax.ShapeDtypeStruct((batch, M_HEADS, LANES), F32),
        jax.ShapeDtypeStruct((batch, H_HEADS, hdk, hdk), F32),
    ]
    x_spec = pl.BlockSpec((nb, tc, d_model), x_map)
    in_specs = [x_spec] + (state_specs if has_init else []) + [_const_spec(w.shape) for w in weights]
    operands = [x] + (list(state) if has_init else []) + list(weights)

    scratch = [
        pltpu.VMEM((nb, tc + HIST, d_m), F32),
        pltpu.VMEM((m_rows, d_model), BF16),
        pltpu.VMEM((m_rows, d_m), F32),
        pltpu.VMEM((m_rows, M_HEADS * dk), F32),
        pltpu.VMEM((m_rows, M_HEADS * dk), F32),
        pltpu.VMEM((m_rows, d_m), op_dtype),
        pltpu.VMEM((m_rows, LANES), F32),
        pltpu.VMEM((m_rows, d_m), F32),
        pltpu.VMEM((m_rows, d_m), F32),
        pltpu.VMEM((m_rows, d_m), F32),
        pltpu.VMEM((m_rows, d_m), F32),
        pltpu.VMEM((m_rows, d_m), op_dtype),
        pltpu.VMEM((m_rows, d_m), F32),
    ]
    outs = pl.pallas_call(
        functools.partial(_mixer_kernel, nb, tc, lh, has_init),
        grid=grid,
        in_specs=in_specs,
        out_specs=[x_spec] + state_specs,
        out_shape=[jax.ShapeDtypeStruct(x.shape, x.dtype)] + state_shapes,
        scratch_shapes=scratch,
        compiler_params=pltpu.CompilerParams(dimension_semantics=semantics,
                                             vmem_limit_bytes=VMEM_LIMIT_BYTES),
        name="mixer_sample" if has_init else "mixer_prompt",
    )(*operands)
    return outs


def kernel(x_prompt, x_sample, state_mlstm_conv, state_mlstm_C, state_mlstm_n, state_mlstm_m,
           state_hgrn_S, g_norm, w_in, conv_w, conv_b, w_q, w_k, w_v, w_gate, b_gate, m_ln,
           m_skip, lb_param, h_norm, w_out, g_final):
    depth = w_in.shape[0]
    assert depth == 1 and lb_param.shape[0] == 2
    d_m = conv_w.shape[-1]
    n_gate = w_gate.shape[-1]
    weights = (
        g_norm[0][None, :],
        w_in[0].astype(BF16),
        conv_w[0],
        conv_b[0][None, :],
        w_q[0].astype(BF16),
        w_k[0].astype(BF16),
        w_v[0].astype(BF16),
        jnp.pad(w_gate[0], ((0, 0), (0, LANES - n_gate))).astype(BF16),
        jnp.pad(b_gate[0], (0, LANES - n_gate))[None, :],
        m_ln[0].reshape(1, d_m),
        m_skip[0][None, :],
        lb_param,
        h_norm[0].reshape(1, d_m),
        w_out[0].astype(BF16),
        g_final[None, :],
    )
    y_p, p_conv, p_c, p_n, p_m, p_s = _mixer_pass(x_prompt, None, weights, nb=4, tc=PROMPT_CHUNK)
    m_lanes = jnp.broadcast_to(state_mlstm_m[0][:, :, None], state_mlstm_m.shape[1:] + (LANES,))
    state = (state_mlstm_conv[0], state_mlstm_C[0], state_mlstm_n[0], m_lanes, state_hgrn_S[0])
    y_s, s_conv, s_c, s_n, s_m, s_s = _mixer_pass(x_sample, state, weights, nb=4,
                                                  tc=x_sample.shape[1])
    return (y_p, y_s, p_conv[None], p_c[None], p_n[None], p_m[None, :, :, 0], p_s[None],
            s_conv[None], s_c[None], s_n[None], s_m[None, :, :, 0], s_s[None])
```

```python
import functools

import jax
import jax.numpy as jnp
from jax import lax
from jax.experimental import pallas as pl
from jax.experimental.pallas import tpu as pltpu

F32 = jnp.float32
BF16 = jnp.bfloat16

EPS = 1e-6
CONV_W = 4
M_HEADS = 4
H_HEADS = 8
N_SPLITS = 7
LANES = 128
MXU_COLS = 256
SUBLANES = 8
HIST = SUBLANES
PROMPT_CHUNK = 64
HGRN_SUBCHUNK = 32
VMEM_LIMIT_BYTES = 56 * 1024 * 1024


def _sigmoid(x):
    return 1.0 / (1.0 + jnp.exp(-x))


def _silu(x):
    return x * _sigmoid(x)


def _log_sigmoid(x):
    return jnp.minimum(x, 0.0) - jnp.log(1.0 + jnp.exp(-jnp.abs(x)))


def _dot(a, b):
    return jnp.dot(a, b, preferred_element_type=F32)


def _dot_nt(a, b):
    return lax.dot_general(a, b, (((1,), (1,)), ((), ())), preferred_element_type=F32)


def _dot_tn(a, b):
    return lax.dot_general(a, b, (((0,), (0,)), ((), ())), preferred_element_type=F32)


def _transpose_rows(a):
    r = a.shape[0]
    if r < LANES:
        a = jnp.concatenate([a, jnp.zeros((LANES - r, LANES), a.dtype)], axis=0)
    return a.T


def _round_robin(gens):
    gens = list(gens)
    while gens:
        for g in list(gens):
            try:
                next(g)
            except StopIteration:
                gens.remove(g)
        yield


def _mixer_kernel(nb, tc, lh, has_init, *refs):
    if has_init:
        (x_ref, conv0_ref, c0_ref, n0_ref, m0_ref, s0_ref), refs = refs[:6], refs[6:]
    else:
        x_ref, refs = refs[0], refs[1:]
    (g_norm_ref, w_in_ref, conv_w_ref, conv_b_ref, w_q_ref, w_k_ref, w_v_ref, w_gate_ref,
     b_gate_ref, m_ln_ref, m_skip_ref, lb_param_ref, h_norm_ref, w_out_ref, g_final_ref,
     y_ref, conv_o_ref, c_o_ref, n_o_ref, m_o_ref, s_o_ref,
     convbuf, xn_s, xc_s, q_s, k_s, v_s, g_s, hm_s, logf_s, kh_s, qs_s, ih_s, oh_s,
     gact_s) = refs

    m_rows = nb * tc
    d_model = x_ref.shape[-1]
    d_m = convbuf.shape[-1]
    dk = w_q_ref.shape[-1]
    dv = w_v_ref.shape[-1]
    hdk = d_m // H_HEADS
    q_scale = dk ** -0.5

    if has_init:
        c_i_ref, n_i_ref, m_i_ref, s_i_ref = c0_ref, n0_ref, m0_ref, s0_ref
        convbuf[:, HIST - (CONV_W - 1):HIST, :] = conv0_ref[...]
    else:
        c_i_ref, n_i_ref, m_i_ref, s_i_ref = c_o_ref, n_o_ref, m_o_ref, s_o_ref
        t_idx = pl.program_id(1)

        @pl.when(t_idx == 0)
        def _():
            c_o_ref[...] = jnp.zeros_like(c_o_ref)
            n_o_ref[...] = jnp.zeros_like(n_o_ref)
            m_o_ref[...] = jnp.zeros_like(m_o_ref)
            s_o_ref[...] = jnp.zeros_like(s_o_ref)
            convbuf[:, 0:HIST, :] = jnp.zeros((nb, HIST, d_m), F32)

        @pl.when(t_idx > 0)
        def _():
            convbuf[:, 0:HIST, :] = convbuf[:, tc:tc + HIST, :]

    x = x_ref[...].reshape(m_rows, d_model)
    xn = (x * lax.rsqrt(jnp.mean(x * x, -1, keepdims=True) + EPS) * g_norm_ref[...]).astype(BF16)
    xn_s[...] = xn

    def proj(i):
        return _dot(xn, w_in_ref[:, i * d_m:(i + 1) * d_m])

    xm = proj(0)
    convbuf[:, HIST:HIST + tc, :] = xm.reshape(nb, tc, d_m)
    acc = conv_b_ref[...].reshape(1, 1, d_m)
    for j in range(CONV_W):
        lo = HIST - (CONV_W - 1) + j
        acc = acc + convbuf[:, lo:lo + tc, :] * conv_w_ref[j:j + 1, :].reshape(1, 1, d_m)
    conv_o_ref[...] = convbuf[:, tc + HIST - (CONV_W - 1):tc + HIST, :]
    xc = _silu(acc).reshape(m_rows, d_m)
    xc_s[...] = xc

    xm_b = xm.astype(BF16)
    xc_b = xc.astype(BF16)
    for h in range(M_HEADS):
        xc_h = xc_b[:, h * dv:(h + 1) * dv]
        q_s[:, h * dk:(h + 1) * dk] = _dot(xc_h, w_q_ref[h])
        k_s[:, h * dk:(h + 1) * dk] = _dot(xc_h, w_k_ref[h])
        v_s[:, h * dv:(h + 1) * dv] = _dot(xm_b[:, h * dv:(h + 1) * dv], w_v_ref[h]).astype(v_s.dtype)
    n_qk = M_HEADS * dk
    g_s[...] = (b_gate_ref[...]
                + _dot(q_s[...].astype(BF16), w_gate_ref[0:n_qk, :])
                + _dot(k_s[...].astype(BF16), w_gate_ref[n_qk:2 * n_qk, :])
                + _dot(v_s[...].astype(BF16), w_gate_ref[2 * n_qk:, :]))

    lbp = lb_param_ref[...]
    lbe = jnp.exp(lbp - jnp.max(lbp, axis=0, keepdims=True))
    lb = lbe[0:1, :] / jnp.sum(lbe, axis=0, keepdims=True)
    sg = _sigmoid(proj(3))
    logf_s[...] = jnp.log(lb + (1.0 - lb) * sg)
    kh_s[...] = (1.0 - lb) * (1.0 - sg)
    qs_s[...] = _silu(proj(4))
    ih_s[...] = proj(5).astype(ih_s.dtype)

    row_i = lax.broadcasted_iota(jnp.int32, (tc, tc), 0)
    col_i = lax.broadcasted_iota(jnp.int32, (tc, tc), 1)
    causal = col_i <= row_i
    causal_t = row_i <= col_i
    row_h = lax.broadcasted_iota(jnp.int32, (lh, lh), 0)
    col_h = lax.broadcasted_iota(jnp.int32, (lh, lh), 1)
    causal_h = col_h <= row_h
    tril_h = causal_h.astype(F32).astype(BF16)
    neg_inf = F32(-jnp.inf)

    def mlstm_seq(b):
        rows = pl.ds(b * tc, tc)
        gb = g_s[rows, :]
        gt = _transpose_rows(gb)
        lf_cols = _log_sigmoid(gb)
        lf_rows = _log_sigmoid(gt[M_HEADS:2 * M_HEADS, 0:tc])
        yield
        yield from _round_robin([mlstm_head(b, h, gb, gt, lf_cols, lf_rows) for h in range(M_HEADS)])

    def mlstm_head(b, h, gb, gt, lf_cols, lf_rows):
        rows = pl.ds(b * tc, tc)
        i_row = gt[h:h + 1, 0:tc]
        i_col = gb[:, h:h + 1]
        lf_row = lf_rows[h:h + 1, :]
        lf_col = lf_cols[:, M_HEADS + h:M_HEADS + h + 1]
        b_col = jnp.sum(jnp.where(causal, lf_row, 0.0), axis=1, keepdims=True)
        b_row = jnp.sum(jnp.where(causal_t, lf_col, 0.0), axis=0, keepdims=True)
        q = q_s[rows, h * dk:(h + 1) * dk] * q_scale
        k = k_s[rows, h * dk:(h + 1) * dk]
        v_b = v_s[rows, h * dv:(h + 1) * dv].astype(BF16)
        q_b = q.astype(BF16)
        qk = _dot_nt(q_b, k.astype(BF16))
        c_prev = c_i_ref[b, h]
        n_prev = n_i_ref[b, h:h + 1, :]
        qc = _dot(q_b, c_prev.astype(BF16))
        qn = jnp.sum(q * n_prev, axis=1, keepdims=True)
        yield
        m_prev = m_i_ref[b, h:h + 1, 0:1]
        a_row = i_row - b_row
        a_col = i_col - b_col
        log_d = jnp.where(causal, a_row, neg_inf)
        m_col = jnp.maximum(jnp.max(log_d, axis=1, keepdims=True), m_prev)
        yield
        d_mat = jnp.exp(log_d - m_col)
        g_col = jnp.exp(m_prev - m_col)
        s_mat = qk * d_mat
        num = _dot(s_mat.astype(BF16), v_b) + g_col * qc
        den = jnp.sum(s_mat, axis=1, keepdims=True) + g_col * qn
        m_last = m_col[tc - 1:tc, :]
        decay = jnp.exp(m_prev - m_last)
        kw = k * jnp.exp(a_col - m_last)
        c_o_ref[b, h] = decay * c_prev + _dot_tn(kw.astype(BF16), v_b)
        n_o_ref[b, h:h + 1, :] = decay * n_prev + jnp.sum(kw, axis=0, keepdims=True)
        m_o_ref[b, h:h + 1, :] = jnp.broadcast_to(b_col[tc - 1:tc, :] + m_last, (1, LANES))
        yield
        m_t = b_col + m_col
        hval = num * (1.0 / jnp.maximum(jnp.abs(den), jnp.exp(-m_t)))
        mu = jnp.mean(hval, axis=1, keepdims=True)
        yield
        hc = hval - mu
        var = jnp.mean(hc * hc, axis=1, keepdims=True)
        yield
        hm_s[rows, h * dv:(h + 1) * dv] = (hc * lax.rsqrt(var + EPS)
                                           * m_ln_ref[:, h * dv:(h + 1) * dv])

    def hgrn_seq(b):
        for j in range(tc // lh):
            rows_j = pl.ds(b * tc + j * lh, lh)
            lf = logf_s[rows_j, :]
            lf_hi = lf.astype(BF16)
            lf_lo = (lf - lf_hi.astype(F32)).astype(BF16)
            bc = _dot(tril_h, lf_hi) + _dot(tril_h, lf_lo)
            yield
            b_mid = bc[lh // 2 - 1:lh // 2, :]
            b_last = bc[lh - 1:lh, :]
            qs = qs_s[rows_j, :]
            kh = kh_s[rows_j, :]
            p_all = (qs * jnp.exp(bc - b_mid)).astype(BF16)
            q_all = (kh * jnp.exp(b_mid - bc)).astype(BF16)
            qe_all = (qs * jnp.exp(bc)).astype(BF16)
            kd_all = (kh * jnp.exp(b_last - bc)).astype(BF16)
            e8 = jnp.concatenate([b_last[:, h * hdk:(h + 1) * hdk] for h in range(H_HEADS)], axis=0)
            dec_cols = jnp.exp(_transpose_rows(e8))
            yield
            yield from _round_robin([hgrn_head(b, h, rows_j, p_all, q_all, qe_all, kd_all, dec_cols)
                                     for h in range(H_HEADS)])

    def hgrn_head(b, h, rows_j, p_all, q_all, qe_all, kd_all, dec_cols):
        sl = slice(h * hdk, (h + 1) * hdk)
        v_b = ih_s[rows_j, sl].astype(BF16)
        s_prev = s_i_ref[b, h]
        pq = _dot_nt(p_all[:, sl], q_all[:, sl])
        qs_ = _dot(qe_all[:, sl], s_prev.astype(BF16))
        s_o_ref[b, h] = dec_cols[:, h:h + 1] * s_prev + _dot_tn(kd_all[:, sl], v_b)
        yield
        a_mat = jnp.where(causal_h, pq, 0.0)
        o = _dot(a_mat.astype(BF16), v_b) + qs_
        yield
        ms = jnp.mean(o * o, axis=1, keepdims=True)
        yield
        oh_s[rows_j, sl] = o * lax.rsqrt(ms + EPS) * h_norm_ref[:, sl]

    gate_acts = ((1, _silu), (2, _sigmoid), (6, _silu))

    def gate_pieces():
        for slot, (sec, act) in enumerate(gate_acts):
            for c in range(d_m // MXU_COLS):
                col = sec * d_m + c * MXU_COLS
                gact_s[slot, :, c * MXU_COLS:(c + 1) * MXU_COLS] = act(
                    _dot(xn_s[...], w_in_ref[:, col:col + MXU_COLS]))
                yield

    for _ in _round_robin([mlstm_seq(b) for b in range(nb)] + [hgrn_seq(b) for b in range(nb)]
                          + [gate_pieces()]):
        pass

    hm = hm_s[...] * gact_s[1]
    hm = (hm + m_skip_ref[...] * xc_s[...]) * gact_s[0]
    oh = oh_s[...] * gact_s[2]
    y = (x_ref[...].reshape(m_rows, d_model)
         + _dot(hm.astype(BF16), w_out_ref[0:d_m, :])
         + _dot(oh.astype(BF16), w_out_ref[d_m:, :]))
    y = y * lax.rsqrt(jnp.mean(y * y, -1, keepdims=True) + EPS) * g_final_ref[...]
    y_ref[...] = y.reshape(nb, tc, d_model)


def _const_spec(shape):
    zeros = (0,) * len(shape)
    return pl.BlockSpec(shape, lambda *_: zeros, pipeline_mode=pl.Buffered(1))


def _mixer_pass(x, state, weights, nb, tc):
    batch, seq, d_model = x.shape
    (g_norm, w_in, conv_w, conv_b, w_q, w_k, w_v, w_gate, b_gate, m_ln, m_skip, lb_param,
     h_norm, w_out, g_final) = weights
    d_m = conv_w.shape[-1]
    dk, dv = w_q.shape[-1], w_v.shape[-1]
    hdk = d_m // H_HEADS
    has_init = state is not None
    lh = min(tc, HGRN_SUBCHUNK)
    m_rows = nb * tc
    op_dtype = BF16 if tc % (2 * SUBLANES) == 0 else F32

    if has_init:
        assert seq == tc
        grid = (batch // nb,)
        bmap = lambda g: (g, 0, 0)
        bmap4 = lambda g: (g, 0, 0, 0)
        semantics = ("arbitrary",)
    else:
        grid = (batch // nb, seq // tc)
        bmap = lambda g, t: (g, 0, 0)
        bmap4 = lambda g, t: (g, 0, 0, 0)
        semantics = ("arbitrary", "arbitrary")
    x_map = (lambda g: (g, 0, 0)) if has_init else (lambda g, t: (g, t, 0))

    state_specs = [
        pl.BlockSpec((nb, CONV_W - 1, d_m), bmap),
        pl.BlockSpec((nb, M_HEADS, dk, dv), bmap4),
        pl.BlockSpec((nb, M_HEADS, dk), bmap),
        pl.BlockSpec((nb, M_HEADS, LANES), bmap),
        pl.BlockSpec((nb, H_HEADS, hdk, hdk), bmap4),
    ]
    state_shapes = [
        jax.ShapeDtypeStruct((batch, CONV_W - 1, d_m), F32),
        jax.ShapeDtypeStruct((batch, M_HEADS, dk, dv), F32),
        jax.ShapeDtypeStruct((batch, M_HEADS, dk), F32),
        jax.ShapeDtypeStruct((batch, M_HEADS, LANES), F32),
        jax.ShapeDtypeStruct((batch, H_HEADS, hdk, hdk), F32),
    ]
    x_spec = pl.BlockSpec((nb, tc, d_model), x_map)
    in_specs = [x_spec] + (state_specs if has_init else []) + [_const_spec(w.shape) for w in weights]
    operands = [x] + (list(state) if has_init else []) + list(weights)

    scratch = [
        pltpu.VMEM((nb, tc + HIST, d_m), F32),
        pltpu.VMEM((m_rows, d_model), BF16),
        pltpu.VMEM((m_rows, d_m), F32),
        pltpu.VMEM((m_rows, M_HEADS * dk), F32),
        pltpu.VMEM((m_rows, M_HEADS * dk), F32),
        pltpu.VMEM((m_rows, d_m), op_dtype),
        pltpu.VMEM((m_rows, LANES), F32),
        pltpu.VMEM((m_rows, d_m), F32),
        pltpu.VMEM((m_rows, d_m), F32),
        pltpu.VMEM((m_rows, d_m), F32),
        pltpu.VMEM((m_rows, d_m), F32),
        pltpu.VMEM((m_rows, d_m), op_dtype),
        pltpu.VMEM((m_rows, d_m), F32),
        pltpu.VMEM((3, m_rows, d_m), F32),
    ]
    outs = pl.pallas_call(
        functools.partial(_mixer_kernel, nb, tc, lh, has_init),
        grid=grid,
        in_specs=in_specs,
        out_specs=[x_spec] + state_specs,
        out_shape=[jax.ShapeDtypeStruct(x.shape, x.dtype)] + state_shapes,
        scratch_shapes=scratch,
        compiler_params=pltpu.CompilerParams(dimension_semantics=semantics,
                                             vmem_limit_bytes=VMEM_LIMIT_BYTES),
        name="mixer_sample" if has_init else "mixer_prompt",
    )(*operands)
    return outs


def kernel(x_prompt, x_sample, state_mlstm_conv, state_mlstm_C, state_mlstm_n, state_mlstm_m,
           state_hgrn_S, g_norm, w_in, conv_w, conv_b, w_q, w_k, w_v, w_gate, b_gate, m_ln,
           m_skip, lb_param, h_norm, w_out, g_final):
    depth = w_in.shape[0]
    assert depth == 1 and lb_param.shape[0] == 2
    d_m = conv_w.shape[-1]
    n_gate = w_gate.shape[-1]
    weights = (
        g_norm[0][None, :],
        w_in[0].astype(BF16),
        conv_w[0],
        conv_b[0][None, :],
        w_q[0].astype(BF16),
        w_k[0].astype(BF16),
        w_v[0].astype(BF16),
        jnp.pad(w_gate[0], ((0, 0), (0, LANES - n_gate))).astype(BF16),
        jnp.pad(b_gate[0], (0, LANES - n_gate))[None, :],
        m_ln[0].reshape(1, d_m),
        m_skip[0][None, :],
        lb_param,
        h_norm[0].reshape(1, d_m),
        w_out[0].astype(BF16),
        g_final[None, :],
    )
    y_p, p_conv, p_c, p_n, p_m, p_s = _mixer_pass(x_prompt, None, weights, nb=4, tc=PROMPT_CHUNK)
    m_lanes = jnp.broadcast_to(state_mlstm_m[0][:, :, None], state_mlstm_m.shape[1:] + (LANES,))
    state = (state_mlstm_conv[0], state_mlstm_C[0], state_mlstm_n[0], m_lanes, state_hgrn_S[0])
    y_s, s_conv, s_c, s_n, s_m, s_s = _mixer_pass(x_sample, state, weights, nb=4,
                                                  tc=x_sample.shape[1])
    return (y_p, y_s, p_conv[None], p_c[None], p_n[None], p_m[None, :, :, 0], p_s[None],
            s_conv[None], s_c[None], s_n[None], s_m[None, :, :, 0], s_s[None])
```

```python
import collections
import functools

import jax
import jax.numpy as jnp
from jax import lax
from jax.experimental import pallas as pl
from jax.experimental.pallas import tpu as pltpu

F32 = jnp.float32
BF16 = jnp.bfloat16

EPS = 1e-6
CONV_W = 4
M_HEADS = 4
H_HEADS = 8
LANES = 128
SUBLANES = 8
MXU_COLS = 256
HIST = SUBLANES
PROMPT_CHUNK = 64
PROMPT_GROUP = 4
HGRN_SUBCHUNK = 32
VMEM_LIMIT_BYTES = 60 * 1024 * 1024

_Group = collections.namedtuple(
    "_Group", "row0 n_seq tc lh convbuf c_in n_in m_in s_in c_out n_out m_out s_out")


def _sigmoid(x):
    return 1.0 / (1.0 + jnp.exp(-x))


def _silu(x):
    return x * _sigmoid(x)


def _log_sigmoid(x):
    return jnp.minimum(x, 0.0) - jnp.log(1.0 + jnp.exp(-jnp.abs(x)))


def _dot(a, b):
    return jnp.dot(a, b, preferred_element_type=F32)


def _dot_nt(a, b):
    return lax.dot_general(a, b, (((1,), (1,)), ((), ())), preferred_element_type=F32)


def _dot_tn(a, b):
    return lax.dot_general(a, b, (((0,), (0,)), ((), ())), preferred_element_type=F32)


def _transpose_rows(a):
    r = a.shape[0]
    if r < LANES:
        a = jnp.concatenate([a, jnp.zeros((LANES - r, LANES), a.dtype)], axis=0)
    return a.T


def _round_robin(gens):
    gens = list(gens)
    while gens:
        for g in list(gens):
            try:
                next(g)
            except StopIteration:
                gens.remove(g)
        yield


def _mixer_kernel(n_p, tc_p, n_s, tc_s, *refs):
    (xp_ref, xs_ref, conv0_ref, c0_ref, n0_ref, m0_ref, s0_ref,
     g_norm_ref, w_in_ref, conv_w_ref, conv_b_ref, w_q_ref, w_k_ref, w_v_ref, w_gate_ref,
     b_gate_ref, m_ln_ref, m_skip_ref, lb_param_ref, h_norm_ref, w_out_ref, g_final_ref,
     yp_ref, ys_ref, pconv_ref, pc_ref, pn_ref, pm_ref, ps_ref,
     sconv_ref, sc_ref, sn_ref, sm_ref, ss_ref,
     convbuf_p, convbuf_s, xn_s, xc_s, q_s, k_s, v_s, g_s, hm_s, logf_s, kh_s, qs_s, ih_s, oh_s,
     gact_s) = refs

    rows_p = n_p * tc_p
    rows_s = n_s * tc_s
    d_model = xp_ref.shape[-1]
    d_m = convbuf_p.shape[-1]
    dk = w_q_ref.shape[-1]
    dv = w_v_ref.shape[-1]
    hdk = d_m // H_HEADS
    q_scale = dk ** -0.5

    prompt = _Group(0, n_p, tc_p, min(tc_p, HGRN_SUBCHUNK), convbuf_p,
                    pc_ref, pn_ref, pm_ref, ps_ref, pc_ref, pn_ref, pm_ref, ps_ref)
    sample = _Group(rows_p, n_s, tc_s, min(tc_s, HGRN_SUBCHUNK), convbuf_s,
                    c0_ref, n0_ref, m0_ref, s0_ref, sc_ref, sn_ref, sm_ref, ss_ref)
    groups = (prompt, sample)

    t_idx = pl.program_id(1)

    @pl.when(t_idx == 0)
    def _():
        pc_ref[...] = jnp.zeros_like(pc_ref)
        pn_ref[...] = jnp.zeros_like(pn_ref)
        pm_ref[...] = jnp.zeros_like(pm_ref)
        ps_ref[...] = jnp.zeros_like(ps_ref)
        convbuf_p[:, 0:HIST, :] = jnp.zeros((n_p, HIST, d_m), F32)

    @pl.when(t_idx > 0)
    def _():
        convbuf_p[:, 0:HIST, :] = convbuf_p[:, tc_p:tc_p + HIST, :]

    convbuf_s[:, HIST - (CONV_W - 1):HIST, :] = conv0_ref[...]

    x = jnp.concatenate([xp_ref[...].reshape(rows_p, d_model),
                         xs_ref[...].reshape(rows_s, d_model)], axis=0)
    xn = (x * lax.rsqrt(jnp.mean(x * x, -1, keepdims=True) + EPS) * g_norm_ref[...]).astype(BF16)
    xn_s[...] = xn

    def proj(i):
        return _dot(xn, w_in_ref[:, i * d_m:(i + 1) * d_m])

    xm = proj(0)
    for grp, conv_o_ref in ((prompt, pconv_ref), (sample, sconv_ref)):
        n_rows = grp.n_seq * grp.tc
        grp.convbuf[:, HIST:HIST + grp.tc, :] = (
            xm[grp.row0:grp.row0 + n_rows].reshape(grp.n_seq, grp.tc, d_m))
        acc = conv_b_ref[...].reshape(1, 1, d_m)
        for j in range(CONV_W):
            lo = HIST - (CONV_W - 1) + j
            acc = acc + grp.convbuf[:, lo:lo + grp.tc, :] * conv_w_ref[j:j + 1, :].reshape(1, 1, d_m)
        conv_o_ref[...] = grp.convbuf[:, grp.tc + HIST - (CONV_W - 1):grp.tc + HIST, :]
        xc_s[grp.row0:grp.row0 + n_rows, :] = _silu(acc).reshape(n_rows, d_m)

    xm_b = xm.astype(BF16)
    xc_b = xc_s[...].astype(BF16)
    for h in range(M_HEADS):
        xc_h = xc_b[:, h * dv:(h + 1) * dv]
        q_s[:, h * dk:(h + 1) * dk] = _dot(xc_h, w_q_ref[h])
        k_s[:, h * dk:(h + 1) * dk] = _dot(xc_h, w_k_ref[h])
        v_s[:, h * dv:(h + 1) * dv] = _dot(xm_b[:, h * dv:(h + 1) * dv], w_v_ref[h])
    n_qk = M_HEADS * dk
    g_s[...] = (b_gate_ref[...]
                + _dot(q_s[...].astype(BF16), w_gate_ref[0:n_qk, :])
                + _dot(k_s[...].astype(BF16), w_gate_ref[n_qk:2 * n_qk, :])
                + _dot(v_s[...].astype(BF16), w_gate_ref[2 * n_qk:, :]))

    lbp = lb_param_ref[...]
    lbe = jnp.exp(lbp - jnp.max(lbp, axis=0, keepdims=True))
    lb = lbe[0:1, :] / jnp.sum(lbe, axis=0, keepdims=True)
    sg = _sigmoid(proj(3))
    logf_s[...] = jnp.log(lb + (1.0 - lb) * sg)
    kh_s[...] = (1.0 - lb) * (1.0 - sg)
    qs_s[...] = _silu(proj(4))
    ih_s[...] = proj(5)

    neg_inf = F32(-jnp.inf)

    def masks(n):
        row = lax.broadcasted_iota(jnp.int32, (n, n), 0)
        col = lax.broadcasted_iota(jnp.int32, (n, n), 1)
        return col <= row, row <= col

    def mlstm_seq(grp, b, causal, causal_t):
        rows = pl.ds(grp.row0 + b * grp.tc, grp.tc)
        gb = g_s[rows, :]
        gt = _transpose_rows(gb)
        lf_cols = _log_sigmoid(gb)
        lf_rows = _log_sigmoid(gt[M_HEADS:2 * M_HEADS, 0:grp.tc])
        yield
        yield from _round_robin([mlstm_head(grp, b, h, rows, causal, causal_t, gb, gt, lf_cols, lf_rows)
                                 for h in range(M_HEADS)])

    def mlstm_head(grp, b, h, rows, causal, causal_t, gb, gt, lf_cols, lf_rows):
        tc = grp.tc
        i_row = gt[h:h + 1, 0:tc]
        i_col = gb[:, h:h + 1]
        lf_row = lf_rows[h:h + 1, :]
        lf_col = lf_cols[:, M_HEADS + h:M_HEADS + h + 1]
        b_col = jnp.sum(jnp.where(causal, lf_row, 0.0), axis=1, keepdims=True)
        b_row = jnp.sum(jnp.where(causal_t, lf_col, 0.0), axis=0, keepdims=True)
        q = q_s[rows, h * dk:(h + 1) * dk] * q_scale
        k = k_s[rows, h * dk:(h + 1) * dk]
        v_b = v_s[rows, h * dv:(h + 1) * dv].astype(BF16)
        q_b = q.astype(BF16)
        qk = _dot_nt(q_b, k.astype(BF16))
        c_prev = grp.c_in[b, h]
        n_prev = grp.n_in[b, h:h + 1, :]
        qc = _dot(q_b, c_prev.astype(BF16))
        qn = jnp.sum(q * n_prev, axis=1, keepdims=True)
        yield
        m_prev = grp.m_in[b, h:h + 1, 0:1]
        a_row = i_row - b_row
        a_col = i_col - b_col
        log_d = jnp.where(causal, a_row, neg_inf)
        m_col = jnp.maximum(jnp.max(log_d, axis=1, keepdims=True), m_prev)
        yield
        d_mat = jnp.exp(log_d - m_col)
        g_col = jnp.exp(m_prev - m_col)
        s_mat = qk * d_mat
        num = _dot(s_mat.astype(BF16), v_b) + g_col * qc
        den = jnp.sum(s_mat, axis=1, keepdims=True) + g_col * qn
        m_last = m_col[tc - 1:tc, :]
        decay = jnp.exp(m_prev - m_last)
        kw = k * jnp.exp(a_col - m_last)
        grp.c_out[b, h] = decay * c_prev + _dot_tn(kw.astype(BF16), v_b)
        grp.n_out[b, h:h + 1, :] = decay * n_prev + jnp.sum(kw, axis=0, keepdims=True)
        grp.m_out[b, h:h + 1, :] = jnp.broadcast_to(b_col[tc - 1:tc, :] + m_last, (1, LANES))
        yield
        m_t = b_col + m_col
        hval = num * (1.0 / jnp.maximum(jnp.abs(den), jnp.exp(-m_t)))
        mu = jnp.mean(hval, axis=1, keepdims=True)
        yield
        hc = hval - mu
        var = jnp.mean(hc * hc, axis=1, keepdims=True)
        yield
        hm_s[rows, h * dv:(h + 1) * dv] = (hc * lax.rsqrt(var + EPS)
                                           * m_ln_ref[:, h * dv:(h + 1) * dv])

    def hgrn_seq(grp, b, causal_h):
        lh = grp.lh
        tril_h = causal_h.astype(F32).astype(BF16)
        for j in range(grp.tc // lh):
            rows_j = pl.ds(grp.row0 + b * grp.tc + j * lh, lh)
            lf = logf_s[rows_j, :]
            lf_hi = lf.astype(BF16)
            lf_lo = (lf - lf_hi.astype(F32)).astype(BF16)
            bc = _dot(tril_h, lf_hi) + _dot(tril_h, lf_lo)
            yield
            b_mid = bc[lh // 2 - 1:lh // 2, :]
            b_last = bc[lh - 1:lh, :]
            qs = qs_s[rows_j, :]
            kh = kh_s[rows_j, :]
            p_all = (qs * jnp.exp(bc - b_mid)).astype(BF16)
            q_all = (kh * jnp.exp(b_mid - bc)).astype(BF16)
            qe_all = (qs * jnp.exp(bc)).astype(BF16)
            kd_all = (kh * jnp.exp(b_last - bc)).astype(BF16)
            e8 = jnp.concatenate([b_last[:, h * hdk:(h + 1) * hdk] for h in range(H_HEADS)], axis=0)
            dec_cols = jnp.exp(_transpose_rows(e8))
            yield
            s_in = grp.s_in if j == 0 else grp.s_out
            yield from _round_robin([hgrn_head(grp, s_in, b, h, rows_j, causal_h, p_all, q_all, qe_all,
                                               kd_all, dec_cols) for h in range(H_HEADS)])

    def hgrn_head(grp, s_in, b, h, rows_j, causal_h, p_all, q_all, qe_all, kd_all, dec_cols):
        sl = slice(h * hdk, (h + 1) * hdk)
        v_b = ih_s[rows_j, sl].astype(BF16)
        s_prev = s_in[b, h]
        pq = _dot_nt(p_all[:, sl], q_all[:, sl])
        qs_ = _dot(qe_all[:, sl], s_prev.astype(BF16))
        grp.s_out[b, h] = dec_cols[:, h:h + 1] * s_prev + _dot_tn(kd_all[:, sl], v_b)
        yield
        a_mat = jnp.where(causal_h, pq, 0.0)
        o = _dot(a_mat.astype(BF16), v_b) + qs_
        yield
        ms = jnp.mean(o * o, axis=1, keepdims=True)
        yield
        oh_s[rows_j, sl] = o * lax.rsqrt(ms + EPS) * h_norm_ref[:, sl]

    gate_acts = ((1, _silu), (2, _sigmoid), (6, _silu))

    def gate_pieces():
        for slot, (sec, act) in enumerate(gate_acts):
            for c in range(d_m // MXU_COLS):
                col = sec * d_m + c * MXU_COLS
                gact_s[slot, :, c * MXU_COLS:(c + 1) * MXU_COLS] = act(
                    _dot(xn_s[...], w_in_ref[:, col:col + MXU_COLS]))
                yield

    units = [gate_pieces()]
    for grp in groups:
        causal, causal_t = masks(grp.tc)
        causal_h, _ = masks(grp.lh)
        units += [mlstm_seq(grp, b, causal, causal_t) for b in range(grp.n_seq)]
        units += [hgrn_seq(grp, b, causal_h) for b in range(grp.n_seq)]
    for _ in _round_robin(units):
        pass

    hm = hm_s[...] * gact_s[1]
    hm = (hm + m_skip_ref[...] * xc_s[...]) * gact_s[0]
    oh = oh_s[...] * gact_s[2]
    y = (jnp.concatenate([xp_ref[...].reshape(rows_p, d_model),
                          xs_ref[...].reshape(rows_s, d_model)], axis=0)
         + _dot(hm.astype(BF16), w_out_ref[0:d_m, :])
         + _dot(oh.astype(BF16), w_out_ref[d_m:, :]))
    y = y * lax.rsqrt(jnp.mean(y * y, -1, keepdims=True) + EPS) * g_final_ref[...]
    yp_ref[...] = y[0:rows_p].reshape(n_p, tc_p, d_model)
    ys_ref[...] = y[rows_p:].reshape(n_s, tc_s, d_model)


def _const_spec(shape):
    zeros = (0,) * len(shape)
    return pl.BlockSpec(shape, lambda *_: zeros, pipeline_mode=pl.Buffered(1))


def _state_shapes(batch, d_m, dk, dv, hdk):
    return [
        jax.ShapeDtypeStruct((batch, CONV_W - 1, d_m), F32),
        jax.ShapeDtypeStruct((batch, M_HEADS, dk, dv), F32),
        jax.ShapeDtypeStruct((batch, M_HEADS, dk), F32),
        jax.ShapeDtypeStruct((batch, M_HEADS, LANES), F32),
        jax.ShapeDtypeStruct((batch, H_HEADS, hdk, hdk), F32),
    ]


def _state_specs(nb, d_m, dk, dv, hdk, batch_index):
    def spec(shape):
        zeros = (0,) * (len(shape) - 1)
        return pl.BlockSpec(shape, lambda g, t: (batch_index(g, t),) + zeros)
    return [spec((nb, CONV_W - 1, d_m)), spec((nb, M_HEADS, dk, dv)), spec((nb, M_HEADS, dk)),
            spec((nb, M_HEADS, LANES)), spec((nb, H_HEADS, hdk, hdk))]


def _mixer_call(x_p, x_s, state_s, weights):
    b_p, t_p, d_model = x_p.shape
    b_s, tc_s, _ = x_s.shape
    (g_norm, w_in, conv_w, conv_b, w_q, w_k, w_v, w_gate, b_gate, m_ln, m_skip, lb_param,
     h_norm, w_out, g_final) = weights
    d_m = conv_w.shape[-1]
    dk, dv = w_q.shape[-1], w_v.shape[-1]
    hdk = d_m // H_HEADS
    n_p, tc_p = PROMPT_GROUP, PROMPT_CHUNK
    n_groups, n_steps = b_p // n_p, t_p // tc_p
    n_s = b_s // (n_groups * n_steps)
    assert n_groups * n_p == b_p and n_steps * tc_p == t_p and n_s * n_groups * n_steps == b_s
    rows = n_p * tc_p + n_s * tc_s

    xp_spec = pl.BlockSpec((n_p, tc_p, d_model), lambda g, t: (g, t, 0))
    xs_spec = pl.BlockSpec((n_s, tc_s, d_model), lambda g, t: (g * n_steps + t, 0, 0))
    p_state = _state_specs(n_p, d_m, dk, dv, hdk, lambda g, t: g)
    s_state = _state_specs(n_s, d_m, dk, dv, hdk, lambda g, t: g * n_steps + t)

    scratch = [
        pltpu.VMEM((n_p, tc_p + HIST, d_m), F32),
        pltpu.VMEM((n_s, tc_s + HIST, d_m), F32),
        pltpu.VMEM((rows, d_model), BF16),
        pltpu.VMEM((rows, d_m), F32),
        pltpu.VMEM((rows, M_HEADS * dk), F32),
        pltpu.VMEM((rows, M_HEADS * dk), F32),
        pltpu.VMEM((rows, d_m), F32),
        pltpu.VMEM((rows, LANES), F32),
        pltpu.VMEM((rows, d_m), F32),
        pltpu.VMEM((rows, d_m), F32),
        pltpu.VMEM((rows, d_m), F32),
        pltpu.VMEM((rows, d_m), F32),
        pltpu.VMEM((rows, d_m), F32),
        pltpu.VMEM((rows, d_m), F32),
        pltpu.VMEM((3, rows, d_m), F32),
    ]
    return pl.pallas_call(
        functools.partial(_mixer_kernel, n_p, tc_p, n_s, tc_s),
        grid=(n_groups, n_steps),
        in_specs=[xp_spec, xs_spec] + s_state + [_const_spec(w.shape) for w in weights],
        out_specs=[xp_spec, xs_spec] + p_state + s_state,
        out_shape=([jax.ShapeDtypeStruct(x_p.shape, x_p.dtype),
                    jax.ShapeDtypeStruct(x_s.shape, x_s.dtype)]
                   + _state_shapes(b_p, d_m, dk, dv, hdk) + _state_shapes(b_s, d_m, dk, dv, hdk)),
        scratch_shapes=scratch,
        compiler_params=pltpu.CompilerParams(dimension_semantics=("arbitrary", "arbitrary"),
                                             vmem_limit_bytes=VMEM_LIMIT_BYTES),
        name="mixer_step",
    )(x_p, x_s, *state_s, *weights)


def kernel(x_prompt, x_sample, state_mlstm_conv, state_mlstm_C, state_mlstm_n, state_mlstm_m,
           state_hgrn_S, g_norm, w_in, conv_w, conv_b, w_q, w_k, w_v, w_gate, b_gate, m_ln,
           m_skip, lb_param, h_norm, w_out, g_final):
    depth = w_in.shape[0]
    assert depth == 1 and lb_param.shape[0] == 2
    d_m = conv_w.shape[-1]
    n_gate = w_gate.shape[-1]
    weights = (
        g_norm[0][None, :],
        w_in[0].astype(BF16),
        conv_w[0],
        conv_b[0][None, :],
        w_q[0].astype(BF16),
        w_k[0].astype(BF16),
        w_v[0].astype(BF16),
        jnp.pad(w_gate[0], ((0, 0), (0, LANES - n_gate))).astype(BF16),
        jnp.pad(b_gate[0], (0, LANES - n_gate))[None, :],
        m_ln[0].reshape(1, d_m),
        m_skip[0][None, :],
        lb_param,
        h_norm[0].reshape(1, d_m),
        w_out[0].astype(BF16),
        g_final[None, :],
    )
    m_lanes = jnp.broadcast_to(state_mlstm_m[0][:, :, None], state_mlstm_m.shape[1:] + (LANES,))
    state = (state_mlstm_conv[0], state_mlstm_C[0], state_mlstm_n[0], m_lanes, state_hgrn_S[0])
    (y_p, y_s, p_conv, p_c, p_n, p_m, p_s, s_conv, s_c, s_n, s_m, s_s) = _mixer_call(
        x_prompt, x_sample, state, weights)
    return (y_p, y_s, p_conv[None], p_c[None], p_n[None], p_m[None, :, :, 0], p_s[None],
            s_conv[None], s_c[None], s_n[None], s_m[None, :, :, 0], s_s[None])
```

```python
import collections
import functools

import jax
import jax.numpy as jnp
from jax import lax
from jax.experimental import pallas as pl
from jax.experimental.pallas import tpu as pltpu

F32 = jnp.float32
BF16 = jnp.bfloat16

EPS = 1e-6
CONV_W = 4
M_HEADS = 4
H_HEADS = 8
LANES = 128
SUBLANES = 8
MXU_COLS = 256
HIST = SUBLANES
PROMPT_CHUNK = 64
PROMPT_GROUP = 4
HGRN_SUBCHUNK = 32
VMEM_LIMIT_BYTES = 60 * 1024 * 1024
GATE_PIECES_PER_ROUND = (2, 2, 0, 0, 3, 3, 2)

_Group = collections.namedtuple(
    "_Group", "row0 n_seq tc lh convbuf c_in n_in m_in s_in c_out n_out m_out s_out")


def _sigmoid(x):
    return 1.0 / (1.0 + jnp.exp(-x))


def _silu(x):
    return x * _sigmoid(x)


def _log_sigmoid(x):
    return jnp.minimum(x, 0.0) - jnp.log(1.0 + jnp.exp(-jnp.abs(x)))


def _dot(a, b):
    return jnp.dot(a, b, preferred_element_type=F32)


def _dot_nt(a, b):
    return lax.dot_general(a, b, (((1,), (1,)), ((), ())), preferred_element_type=F32)


def _dot_tn(a, b):
    return lax.dot_general(a, b, (((0,), (0,)), ((), ())), preferred_element_type=F32)


def _transpose_rows(a):
    r = a.shape[0]
    if r < LANES:
        a = jnp.concatenate([a, jnp.zeros((LANES - r, LANES), a.dtype)], axis=0)
    return a.T


def _round_robin(gens):
    gens = list(gens)
    while gens:
        for g in list(gens):
            try:
                next(g)
            except StopIteration:
                gens.remove(g)
        yield


def _mixer_kernel(n_p, tc_p, n_s, tc_s, *refs):
    (xp_ref, xs_ref, conv0_ref, c0_ref, n0_ref, m0_ref, s0_ref,
     g_norm_ref, w_in_ref, conv_w_ref, conv_b_ref, w_q_ref, w_k_ref, w_v_ref, w_gate_ref,
     b_gate_ref, m_ln_ref, m_skip_ref, lb_param_ref, h_norm_ref, w_out_ref, g_final_ref,
     yp_ref, ys_ref, pconv_ref, pc_ref, pn_ref, pm_ref, ps_ref,
     sconv_ref, sc_ref, sn_ref, sm_ref, ss_ref,
     convbuf_p, convbuf_s, xn_s, xc_s, q_s, k_s, v_s, g_s, hm_s, logf_s, kh_s, qs_s, ih_s, oh_s,
     gact_s) = refs

    rows_p = n_p * tc_p
    rows_s = n_s * tc_s
    d_model = xp_ref.shape[-1]
    d_m = convbuf_p.shape[-1]
    dk = w_q_ref.shape[-1]
    dv = w_v_ref.shape[-1]
    hdk = d_m // H_HEADS
    q_scale = dk ** -0.5

    prompt = _Group(0, n_p, tc_p, min(tc_p, HGRN_SUBCHUNK), convbuf_p,
                    pc_ref, pn_ref, pm_ref, ps_ref, pc_ref, pn_ref, pm_ref, ps_ref)
    sample = _Group(rows_p, n_s, tc_s, min(tc_s, HGRN_SUBCHUNK), convbuf_s,
                    c0_ref, n0_ref, m0_ref, s0_ref, sc_ref, sn_ref, sm_ref, ss_ref)
    groups = (prompt, sample)

    t_idx = pl.program_id(1)

    @pl.when(t_idx == 0)
    def _():
        pc_ref[...] = jnp.zeros_like(pc_ref)
        pn_ref[...] = jnp.zeros_like(pn_ref)
        pm_ref[...] = jnp.zeros_like(pm_ref)
        ps_ref[...] = jnp.zeros_like(ps_ref)
        convbuf_p[:, 0:HIST, :] = jnp.zeros((n_p, HIST, d_m), F32)

    @pl.when(t_idx > 0)
    def _():
        convbuf_p[:, 0:HIST, :] = convbuf_p[:, tc_p:tc_p + HIST, :]

    convbuf_s[:, HIST - (CONV_W - 1):HIST, :] = conv0_ref[...]

    x = jnp.concatenate([xp_ref[...].reshape(rows_p, d_model),
                         xs_ref[...].reshape(rows_s, d_model)], axis=0)
    xn = (x * lax.rsqrt(jnp.mean(x * x, -1, keepdims=True) + EPS) * g_norm_ref[...]).astype(BF16)
    xn_s[...] = xn

    def proj(i):
        return _dot(xn, w_in_ref[:, i * d_m:(i + 1) * d_m])

    xm = proj(0)
    for grp, conv_o_ref in ((prompt, pconv_ref), (sample, sconv_ref)):
        n_rows = grp.n_seq * grp.tc
        grp.convbuf[:, HIST:HIST + grp.tc, :] = (
            xm[grp.row0:grp.row0 + n_rows].reshape(grp.n_seq, grp.tc, d_m))
        acc = conv_b_ref[...].reshape(1, 1, d_m)
        for j in range(CONV_W):
            lo = HIST - (CONV_W - 1) + j
            acc = acc + grp.convbuf[:, lo:lo + grp.tc, :] * conv_w_ref[j:j + 1, :].reshape(1, 1, d_m)
        conv_o_ref[...] = grp.convbuf[:, grp.tc + HIST - (CONV_W - 1):grp.tc + HIST, :]
        xc_s[grp.row0:grp.row0 + n_rows, :] = _silu(acc).reshape(n_rows, d_m)

    xm_b = xm.astype(BF16)
    xc_b = xc_s[...].astype(BF16)
    for h in range(M_HEADS):
        xc_h = xc_b[:, h * dv:(h + 1) * dv]
        q_s[:, h * dk:(h + 1) * dk] = _dot(xc_h, w_q_ref[h])
        k_s[:, h * dk:(h + 1) * dk] = _dot(xc_h, w_k_ref[h])
        v_s[:, h * dv:(h + 1) * dv] = _dot(xm_b[:, h * dv:(h + 1) * dv], w_v_ref[h])
    n_qk = M_HEADS * dk
    g_s[...] = (b_gate_ref[...]
                + _dot(q_s[...].astype(BF16), w_gate_ref[0:n_qk, :])
                + _dot(k_s[...].astype(BF16), w_gate_ref[n_qk:2 * n_qk, :])
                + _dot(v_s[...].astype(BF16), w_gate_ref[2 * n_qk:, :]))

    lbp = lb_param_ref[...]
    lbe = jnp.exp(lbp - jnp.max(lbp, axis=0, keepdims=True))
    lb = lbe[0:1, :] / jnp.sum(lbe, axis=0, keepdims=True)
    sg = _sigmoid(proj(3))
    logf_s[...] = jnp.log(lb + (1.0 - lb) * sg)
    kh_s[...] = (1.0 - lb) * (1.0 - sg)
    qs_s[...] = _silu(proj(4))
    ih_s[...] = proj(5)

    neg_inf = F32(-jnp.inf)

    def masks(n):
        row = lax.broadcasted_iota(jnp.int32, (n, n), 0)
        col = lax.broadcasted_iota(jnp.int32, (n, n), 1)
        return col <= row, row <= col

    def mlstm_seq(grp, b, causal, causal_t):
        rows = pl.ds(grp.row0 + b * grp.tc, grp.tc)
        gb = g_s[rows, :]
        gt = _transpose_rows(gb)
        lf_cols = _log_sigmoid(gb)
        lf_rows = _log_sigmoid(gt[M_HEADS:2 * M_HEADS, 0:grp.tc])
        yield
        yield from _round_robin([mlstm_head(grp, b, h, rows, causal, causal_t, gb, gt, lf_cols, lf_rows)
                                 for h in range(M_HEADS)])

    def mlstm_head(grp, b, h, rows, causal, causal_t, gb, gt, lf_cols, lf_rows):
        tc = grp.tc
        i_row = gt[h:h + 1, 0:tc]
        i_col = gb[:, h:h + 1]
        lf_row = lf_rows[h:h + 1, :]
        lf_col = lf_cols[:, M_HEADS + h:M_HEADS + h + 1]
        b_col = jnp.sum(jnp.where(causal, lf_row, 0.0), axis=1, keepdims=True)
        b_row = jnp.sum(jnp.where(causal_t, lf_col, 0.0), axis=0, keepdims=True)
        q = q_s[rows, h * dk:(h + 1) * dk] * q_scale
        k = k_s[rows, h * dk:(h + 1) * dk]
        v_b = v_s[rows, h * dv:(h + 1) * dv].astype(BF16)
        q_b = q.astype(BF16)
        qk = _dot_nt(q_b, k.astype(BF16))
        c_prev = grp.c_in[b, h]
        n_prev = grp.n_in[b, h:h + 1, :]
        qc = _dot(q_b, c_prev.astype(BF16))
        qn = jnp.sum(q * n_prev, axis=1, keepdims=True)
        yield
        m_prev = grp.m_in[b, h:h + 1, 0:1]
        a_row = i_row - b_row
        a_col = i_col - b_col
        log_d = jnp.where(causal, a_row, neg_inf)
        m_col = jnp.maximum(jnp.max(log_d, axis=1, keepdims=True), m_prev)
        yield
        d_mat = jnp.exp(log_d - m_col)
        g_col = jnp.exp(m_prev - m_col)
        s_mat = qk * d_mat
        num = _dot(s_mat.astype(BF16), v_b) + g_col * qc
        den = jnp.sum(s_mat, axis=1, keepdims=True) + g_col * qn
        m_last = m_col[tc - 1:tc, :]
        decay = jnp.exp(m_prev - m_last)
        kw = k * jnp.exp(a_col - m_last)
        grp.c_out[b, h] = decay * c_prev + _dot_tn(kw.astype(BF16), v_b)
        grp.n_out[b, h:h + 1, :] = decay * n_prev + jnp.sum(kw, axis=0, keepdims=True)
        grp.m_out[b, h:h + 1, :] = jnp.broadcast_to(b_col[tc - 1:tc, :] + m_last, (1, LANES))
        yield
        m_t = b_col + m_col
        hval = num * (1.0 / jnp.maximum(jnp.abs(den), jnp.exp(-m_t)))
        mu = jnp.mean(hval, axis=1, keepdims=True)
        yield
        hc = hval - mu
        var = jnp.mean(hc * hc, axis=1, keepdims=True)
        yield
        hm_s[rows, h * dv:(h + 1) * dv] = (hc * lax.rsqrt(var + EPS)
                                           * m_ln_ref[:, h * dv:(h + 1) * dv])

    def hgrn_seq(grp, b, causal_h):
        lh = grp.lh
        n_sub = grp.tc // lh
        tril_h = causal_h.astype(F32).astype(BF16)
        rows = [pl.ds(grp.row0 + b * grp.tc + j * lh, lh) for j in range(n_sub)]
        bcs = []
        for j in range(n_sub):
            lf = logf_s[rows[j], :]
            lf_hi = lf.astype(BF16)
            lf_lo = (lf - lf_hi.astype(F32)).astype(BF16)
            bcs.append(_dot(tril_h, lf_hi) + _dot(tril_h, lf_lo))
        yield
        operands = []
        for j, bc in enumerate(bcs):
            b_mid = bc[lh // 2 - 1:lh // 2, :]
            b_last = bc[lh - 1:lh, :]
            qs = qs_s[rows[j], :]
            kh = kh_s[rows[j], :]
            p_all = (qs * jnp.exp(bc - b_mid)).astype(BF16)
            q_all = (kh * jnp.exp(b_mid - bc)).astype(BF16)
            qe_all = (qs * jnp.exp(bc)).astype(BF16)
            kd_all = (kh * jnp.exp(b_last - bc)).astype(BF16)
            e8 = jnp.concatenate([b_last[:, h * hdk:(h + 1) * hdk] for h in range(H_HEADS)], axis=0)
            dec_cols = jnp.exp(_transpose_rows(e8))
            operands.append((rows[j], p_all, q_all, qe_all, kd_all, dec_cols))
        yield
        yield from _round_robin([hgrn_head(grp, b, h, causal_h, operands) for h in range(H_HEADS)])

    def hgrn_head(grp, b, h, causal_h, operands):
        sl = slice(h * hdk, (h + 1) * hdk)
        n_sub = len(operands)
        state = grp.s_in[b, h]
        v_b, pq, inter, o, ms = {}, {}, {}, {}, {}
        for k in range(n_sub + 3):
            if k == 0:
                for j, (rows_j, p_all, q_all, _, _, _) in enumerate(operands):
                    v_b[j] = ih_s[rows_j, sl].astype(BF16)
                    pq[j] = _dot_nt(p_all[:, sl], q_all[:, sl])
            if k < n_sub:
                _, _, _, qe_all, kd_all, dec_cols = operands[k]
                inter[k] = _dot(qe_all[:, sl], state.astype(BF16))
                state = dec_cols[:, h:h + 1] * state + _dot_tn(kd_all[:, sl], v_b[k])
                if k == n_sub - 1:
                    grp.s_out[b, h] = state
            if 0 <= k - 1 < n_sub:
                a_mat = jnp.where(causal_h, pq[k - 1], 0.0)
                o[k - 1] = _dot(a_mat.astype(BF16), v_b[k - 1]) + inter[k - 1]
            if 0 <= k - 2 < n_sub:
                ms[k - 2] = jnp.mean(o[k - 2] * o[k - 2], axis=1, keepdims=True)
            if 0 <= k - 3 < n_sub:
                oh_s[operands[k - 3][0], sl] = (o[k - 3] * lax.rsqrt(ms[k - 3] + EPS)
                                                * h_norm_ref[:, sl])
            if k < n_sub + 2:
                yield

    gate_acts = ((1, _silu), (2, _sigmoid), (6, _silu))
    pieces = [(slot, sec, act, c) for slot, (sec, act) in enumerate(gate_acts)
              for c in range(d_m // MXU_COLS)]
    assert sum(GATE_PIECES_PER_ROUND) == len(pieces)

    def gate_pieces():
        todo = iter(pieces)
        for n in GATE_PIECES_PER_ROUND:
            for _ in range(n):
                slot, sec, act, c = next(todo)
                col = sec * d_m + c * MXU_COLS
                gact_s[slot, :, c * MXU_COLS:(c + 1) * MXU_COLS] = act(
                    _dot(xn_s[...], w_in_ref[:, col:col + MXU_COLS]))
            yield

    units = [gate_pieces()]
    for grp in groups:
        causal, causal_t = masks(grp.tc)
        causal_h, _ = masks(grp.lh)
        units += [mlstm_seq(grp, b, causal, causal_t) for b in range(grp.n_seq)]
        units += [hgrn_seq(grp, b, causal_h) for b in range(grp.n_seq)]
    for _ in _round_robin(units):
        pass

    hm = hm_s[...] * gact_s[1]
    hm = (hm + m_skip_ref[...] * xc_s[...]) * gact_s[0]
    oh = oh_s[...] * gact_s[2]
    y = (jnp.concatenate([xp_ref[...].reshape(rows_p, d_model),
                          xs_ref[...].reshape(rows_s, d_model)], axis=0)
         + _dot(hm.astype(BF16), w_out_ref[0:d_m, :])
         + _dot(oh.astype(BF16), w_out_ref[d_m:, :]))
    y = y * lax.rsqrt(jnp.mean(y * y, -1, keepdims=True) + EPS) * g_final_ref[...]
    yp_ref[...] = y[0:rows_p].reshape(n_p, tc_p, d_model)
    ys_ref[...] = y[rows_p:].reshape(n_s, tc_s, d_model)


def _const_spec(shape):
    zeros = (0,) * len(shape)
    return pl.BlockSpec(shape, lambda *_: zeros, pipeline_mode=pl.Buffered(1))


def _state_shapes(batch, d_m, dk, dv, hdk):
    return [
        jax.ShapeDtypeStruct((batch, CONV_W - 1, d_m), F32),
        jax.ShapeDtypeStruct((batch, M_HEADS, dk, dv), F32),
        jax.ShapeDtypeStruct((batch, M_HEADS, dk), F32),
        jax.ShapeDtypeStruct((batch, M_HEADS, LANES), F32),
        jax.ShapeDtypeStruct((batch, H_HEADS, hdk, hdk), F32),
    ]


def _state_specs(nb, d_m, dk, dv, hdk, batch_index):
    def spec(shape):
        zeros = (0,) * (len(shape) - 1)
        return pl.BlockSpec(shape, lambda g, t: (batch_index(g, t),) + zeros)
    return [spec((nb, CONV_W - 1, d_m)), spec((nb, M_HEADS, dk, dv)), spec((nb, M_HEADS, dk)),
            spec((nb, M_HEADS, LANES)), spec((nb, H_HEADS, hdk, hdk))]


def _mixer_call(x_p, x_s, state_s, weights):
    b_p, t_p, d_model = x_p.shape
    b_s, tc_s, _ = x_s.shape
    (g_norm, w_in, conv_w, conv_b, w_q, w_k, w_v, w_gate, b_gate, m_ln, m_skip, lb_param,
     h_norm, w_out, g_final) = weights
    d_m = conv_w.shape[-1]
    dk, dv = w_q.shape[-1], w_v.shape[-1]
    hdk = d_m // H_HEADS
    n_p, tc_p = PROMPT_GROUP, PROMPT_CHUNK
    n_groups, n_steps = b_p // n_p, t_p // tc_p
    n_s = b_s // (n_groups * n_steps)
    assert n_groups * n_p == b_p and n_steps * tc_p == t_p and n_s * n_groups * n_steps == b_s
    rows = n_p * tc_p + n_s * tc_s

    xp_spec = pl.BlockSpec((n_p, tc_p, d_model), lambda g, t: (g, t, 0))
    xs_spec = pl.BlockSpec((n_s, tc_s, d_model), lambda g, t: (g * n_steps + t, 0, 0))
    p_state = _state_specs(n_p, d_m, dk, dv, hdk, lambda g, t: g)
    s_state = _state_specs(n_s, d_m, dk, dv, hdk, lambda g, t: g * n_steps + t)

    scratch = [
        pltpu.VMEM((n_p, tc_p + HIST, d_m), F32),
        pltpu.VMEM((n_s, tc_s + HIST, d_m), F32),
        pltpu.VMEM((rows, d_model), BF16),
        pltpu.VMEM((rows, d_m), F32),
        pltpu.VMEM((rows, M_HEADS * dk), F32),
        pltpu.VMEM((rows, M_HEADS * dk), F32),
        pltpu.VMEM((rows, d_m), F32),
        pltpu.VMEM((rows, LANES), F32),
        pltpu.VMEM((rows, d_m), F32),
        pltpu.VMEM((rows, d_m), F32),
        pltpu.VMEM((rows, d_m), F32),
        pltpu.VMEM((rows, d_m), F32),
        pltpu.VMEM((rows, d_m), F32),
        pltpu.VMEM((rows, d_m), F32),
        pltpu.VMEM((3, rows, d_m), F32),
    ]
    return pl.pallas_call(
        functools.partial(_mixer_kernel, n_p, tc_p, n_s, tc_s),
        grid=(n_groups, n_steps),
        in_specs=[xp_spec, xs_spec] + s_state + [_const_spec(w.shape) for w in weights],
        out_specs=[xp_spec, xs_spec] + p_state + s_state,
        out_shape=([jax.ShapeDtypeStruct(x_p.shape, x_p.dtype),
                    jax.ShapeDtypeStruct(x_s.shape, x_s.dtype)]
                   + _state_shapes(b_p, d_m, dk, dv, hdk) + _state_shapes(b_s, d_m, dk, dv, hdk)),
        scratch_shapes=scratch,
        compiler_params=pltpu.CompilerParams(dimension_semantics=("arbitrary", "arbitrary"),
                                             vmem_limit_bytes=VMEM_LIMIT_BYTES),
        name="mixer_step",
    )(x_p, x_s, *state_s, *weights)


def kernel(x_prompt, x_sample, state_mlstm_conv, state_mlstm_C, state_mlstm_n, state_mlstm_m,
           state_hgrn_S, g_norm, w_in, conv_w, conv_b, w_q, w_k, w_v, w_gate, b_gate, m_ln,
           m_skip, lb_param, h_norm, w_out, g_final):
    depth = w_in.shape[0]
    assert depth == 1 and lb_param.shape[0] == 2
    d_m = conv_w.shape[-1]
    n_gate = w_gate.shape[-1]
    weights = (
        g_norm[0][None, :],
        w_in[0].astype(BF16),
        conv_w[0],
        conv_b[0][None, :],
        w_q[0].astype(BF16),
        w_k[0].astype(BF16),
        w_v[0].astype(BF16),
        jnp.pad(w_gate[0], ((0, 0), (0, LANES - n_gate))).astype(BF16),
        jnp.pad(b_gate[0], (0, LANES - n_gate))[None, :],
        m_ln[0].reshape(1, d_m),
        m_skip[0][None, :],
        lb_param,
        h_norm[0].reshape(1, d_m),
        w_out[0].astype(BF16),
        g_final[None, :],
    )
    m_lanes = jnp.broadcast_to(state_mlstm_m[0][:, :, None], state_mlstm_m.shape[1:] + (LANES,))
    state = (state_mlstm_conv[0], state_mlstm_C[0], state_mlstm_n[0], m_lanes, state_hgrn_S[0])
    (y_p, y_s, p_conv, p_c, p_n, p_m, p_s, s_conv, s_c, s_n, s_m, s_s) = _mixer_call(
        x_prompt, x_sample, state, weights)
    return (y_p, y_s, p_conv[None], p_c[None], p_n[None], p_m[None, :, :, 0], p_s[None],
            s_conv[None], s_c[None], s_n[None], s_m[None, :, :, 0], s_s[None])
```

```python
import collections
import functools

import jax
import jax.numpy as jnp
from jax import lax
from jax.experimental import pallas as pl
from jax.experimental.pallas import tpu as pltpu

F32 = jnp.float32
BF16 = jnp.bfloat16

EPS = 1e-6
CONV_W = 4
M_HEADS = 4
H_HEADS = 8
LANES = 128
SUBLANES = 8
MXU_COLS = 256
HIST = SUBLANES
PROMPT_CHUNK = 64
PROMPT_GROUP = 4
HGRN_SUBCHUNK = 32
VMEM_LIMIT_BYTES = 60 * 1024 * 1024
GATE_PIECES_PER_ROUND = (2, 2, 0, 0, 3, 3, 2)

_Group = collections.namedtuple(
    "_Group", "row0 n_seq tc lh convbuf c_in n_in m_in s_in c_out n_out m_out s_out")


def _sigmoid(x):
    return 0.5 * jnp.tanh(0.5 * x) + 0.5


def _silu(x):
    return x * _sigmoid(x)


def _log_sigmoid(x):
    return jnp.minimum(x, 0.0) - jnp.log(1.0 + jnp.exp(-jnp.abs(x)))


def _dot(a, b):
    return jnp.dot(a, b, preferred_element_type=F32)


def _dot_nt(a, b):
    return lax.dot_general(a, b, (((1,), (1,)), ((), ())), preferred_element_type=F32)


def _dot_tn(a, b):
    return lax.dot_general(a, b, (((0,), (0,)), ((), ())), preferred_element_type=F32)


def _transpose_rows(a):
    r = a.shape[0]
    if r < LANES:
        a = jnp.concatenate([a, jnp.zeros((LANES - r, LANES), a.dtype)], axis=0)
    return a.T


def _round_robin(gens):
    gens = list(gens)
    while gens:
        for g in list(gens):
            try:
                next(g)
            except StopIteration:
                gens.remove(g)
        yield


def _mixer_kernel(n_p, tc_p, n_s, tc_s, *refs):
    (xp_ref, xs_ref, conv0_ref, c0_ref, n0_ref, m0_ref, s0_ref,
     g_norm_ref, w_in_ref, conv_w_ref, conv_b_ref, w_qk_ref, w_v_ref, w_gate_ref,
     b_gate_ref, m_ln_ref, m_skip_ref, lb_param_ref, h_norm_ref, w_out_ref, g_final_ref,
     yp_ref, ys_ref, pconv_ref, pc_ref, pn_ref, pm_ref, ps_ref,
     sconv_ref, sc_ref, sn_ref, sm_ref, ss_ref,
     convbuf_p, convbuf_s, xn_s, xc_s, q_s, k_s, v_s, g_s, hm_s, logf_s, kh_s, qs_s, ih_s, oh_s,
     gact_s) = refs

    rows_p = n_p * tc_p
    rows_s = n_s * tc_s
    d_model = xp_ref.shape[-1]
    d_m = convbuf_p.shape[-1]
    dk = w_qk_ref.shape[-1] // 2
    dv = w_v_ref.shape[-1]
    hdk = d_m // H_HEADS
    q_scale = dk ** -0.5

    prompt = _Group(0, n_p, tc_p, min(tc_p, HGRN_SUBCHUNK), convbuf_p,
                    pc_ref, pn_ref, pm_ref, ps_ref, pc_ref, pn_ref, pm_ref, ps_ref)
    sample = _Group(rows_p, n_s, tc_s, min(tc_s, HGRN_SUBCHUNK), convbuf_s,
                    c0_ref, n0_ref, m0_ref, s0_ref, sc_ref, sn_ref, sm_ref, ss_ref)
    groups = (prompt, sample)

    t_idx = pl.program_id(1)

    @pl.when(t_idx == 0)
    def _():
        pc_ref[...] = jnp.zeros_like(pc_ref)
        pn_ref[...] = jnp.zeros_like(pn_ref)
        pm_ref[...] = jnp.zeros_like(pm_ref)
        ps_ref[...] = jnp.zeros_like(ps_ref)
        convbuf_p[:, 0:HIST, :] = jnp.zeros((n_p, HIST, d_m), F32)

    @pl.when(t_idx > 0)
    def _():
        convbuf_p[:, 0:HIST, :] = convbuf_p[:, tc_p:tc_p + HIST, :]

    convbuf_s[:, HIST - (CONV_W - 1):HIST, :] = conv0_ref[...]

    x = jnp.concatenate([xp_ref[...].reshape(rows_p, d_model),
                         xs_ref[...].reshape(rows_s, d_model)], axis=0)
    xn = (x * lax.rsqrt(jnp.mean(x * x, -1, keepdims=True) + EPS) * g_norm_ref[...]).astype(BF16)
    xn_s[...] = xn

    def proj(i):
        return _dot(xn, w_in_ref[:, i * d_m:(i + 1) * d_m])

    xm = proj(0)
    for grp, conv_o_ref in ((prompt, pconv_ref), (sample, sconv_ref)):
        n_rows = grp.n_seq * grp.tc
        grp.convbuf[:, HIST:HIST + grp.tc, :] = (
            xm[grp.row0:grp.row0 + n_rows].reshape(grp.n_seq, grp.tc, d_m))
        acc = conv_b_ref[...].reshape(1, 1, d_m)
        for j in range(CONV_W):
            lo = HIST - (CONV_W - 1) + j
            acc = acc + grp.convbuf[:, lo:lo + grp.tc, :] * conv_w_ref[j:j + 1, :].reshape(1, 1, d_m)
        conv_o_ref[...] = grp.convbuf[:, grp.tc + HIST - (CONV_W - 1):grp.tc + HIST, :]
        xc_s[grp.row0:grp.row0 + n_rows, :] = _silu(acc).reshape(n_rows, d_m)

    xm_b = xm.astype(BF16)
    xc_b = xc_s[...].astype(BF16)
    for h in range(M_HEADS):
        qk_h = _dot(xc_b[:, h * dv:(h + 1) * dv], w_qk_ref[h])
        q_s[:, h * dk:(h + 1) * dk] = qk_h[:, 0:dk]
        k_s[:, h * dk:(h + 1) * dk] = qk_h[:, dk:2 * dk]
        v_s[:, h * dv:(h + 1) * dv] = _dot(xm_b[:, h * dv:(h + 1) * dv], w_v_ref[h])
    n_qk = M_HEADS * dk
    g_s[...] = (b_gate_ref[...]
                + _dot(q_s[...].astype(BF16), w_gate_ref[0:n_qk, :])
                + _dot(k_s[...].astype(BF16), w_gate_ref[n_qk:2 * n_qk, :])
                + _dot(v_s[...].astype(BF16), w_gate_ref[2 * n_qk:, :]))

    lbp = lb_param_ref[...]
    lbe = jnp.exp(lbp - jnp.max(lbp, axis=0, keepdims=True))
    lb = lbe[0:1, :] / jnp.sum(lbe, axis=0, keepdims=True)
    sg = _sigmoid(proj(3))
    logf_s[...] = jnp.log(lb + (1.0 - lb) * sg)
    kh_s[...] = (1.0 - lb) * (1.0 - sg)
    qs_s[...] = _silu(proj(4))
    ih_s[...] = proj(5)

    neg_inf = F32(-jnp.inf)

    def masks(n):
        row = lax.broadcasted_iota(jnp.int32, (n, n), 0)
        col = lax.broadcasted_iota(jnp.int32, (n, n), 1)
        return col <= row, row <= col

    def mlstm_seq(grp, b, causal, causal_t):
        rows = pl.ds(grp.row0 + b * grp.tc, grp.tc)
        gb = g_s[rows, :]
        gt = _transpose_rows(gb)
        lf_cols = _log_sigmoid(gb)
        lf_rows = _log_sigmoid(gt[M_HEADS:2 * M_HEADS, 0:grp.tc])
        yield
        yield from _round_robin([mlstm_head(grp, b, h, rows, causal, causal_t, gb, gt, lf_cols, lf_rows)
                                 for h in range(M_HEADS)])

    def mlstm_head(grp, b, h, rows, causal, causal_t, gb, gt, lf_cols, lf_rows):
        tc = grp.tc
        i_row = gt[h:h + 1, 0:tc]
        i_col = gb[:, h:h + 1]
        lf_row = lf_rows[h:h + 1, :]
        lf_col = lf_cols[:, M_HEADS + h:M_HEADS + h + 1]
        b_col = jnp.sum(jnp.where(causal, lf_row, 0.0), axis=1, keepdims=True)
        b_row = jnp.sum(jnp.where(causal_t, lf_col, 0.0), axis=0, keepdims=True)
        q = q_s[rows, h * dk:(h + 1) * dk] * q_scale
        k = k_s[rows, h * dk:(h + 1) * dk]
        v_b = v_s[rows, h * dv:(h + 1) * dv].astype(BF16)
        q_b = q.astype(BF16)
        qk = _dot_nt(q_b, k.astype(BF16))
        c_prev = grp.c_in[b, h]
        n_prev = grp.n_in[b, h:h + 1, :]
        qc = _dot(q_b, c_prev.astype(BF16))
        qn = jnp.sum(q * n_prev, axis=1, keepdims=True)
        yield
        m_prev = grp.m_in[b, h:h + 1, 0:1]
        a_row = i_row - b_row
        a_col = i_col - b_col
        log_d = jnp.where(causal, a_row, neg_inf)
        m_col = jnp.maximum(jnp.max(log_d, axis=1, keepdims=True), m_prev)
        yield
        d_mat = jnp.exp(log_d - m_col)
        g_col = jnp.exp(m_prev - m_col)
        s_mat = qk * d_mat
        num = _dot(s_mat.astype(BF16), v_b) + g_col * qc
        den = jnp.sum(s_mat, axis=1, keepdims=True) + g_col * qn
        m_last = m_col[tc - 1:tc, :]
        decay = jnp.exp(m_prev - m_last)
        kw = k * jnp.exp(a_col - m_last)
        grp.c_out[b, h] = decay * c_prev + _dot_tn(kw.astype(BF16), v_b)
        grp.n_out[b, h:h + 1, :] = decay * n_prev + jnp.sum(kw, axis=0, keepdims=True)
        grp.m_out[b, h:h + 1, :] = jnp.broadcast_to(b_col[tc - 1:tc, :] + m_last, (1, LANES))
        yield
        m_t = b_col + m_col
        hval = num * (1.0 / jnp.maximum(jnp.abs(den), jnp.exp(-m_t)))
        mu = jnp.mean(hval, axis=1, keepdims=True)
        yield
        hc = hval - mu
        var = jnp.mean(hc * hc, axis=1, keepdims=True)
        yield
        hm_s[rows, h * dv:(h + 1) * dv] = (hc * lax.rsqrt(var + EPS)
                                           * m_ln_ref[:, h * dv:(h + 1) * dv])

    def hgrn_seq(grp, b, causal_h):
        lh = grp.lh
        n_sub = grp.tc // lh
        tril_h = causal_h.astype(F32).astype(BF16)
        tril_2 = jnp.concatenate([tril_h, tril_h], axis=1)
        rows = [pl.ds(grp.row0 + b * grp.tc + j * lh, lh) for j in range(n_sub)]
        bcs = []
        for j in range(n_sub):
            lf = logf_s[rows[j], :]
            lf_hi = lf.astype(BF16)
            lf_lo = (lf - lf_hi.astype(F32)).astype(BF16)
            bcs.append(_dot(tril_2, jnp.concatenate([lf_hi, lf_lo], axis=0)))
        yield
        operands = []
        for j, bc in enumerate(bcs):
            b_mid = bc[lh // 2 - 1:lh // 2, :]
            b_last = bc[lh - 1:lh, :]
            qs = qs_s[rows[j], :]
            kh = kh_s[rows[j], :]
            p_all = (qs * jnp.exp(bc - b_mid)).astype(BF16)
            q_all = (kh * jnp.exp(b_mid - bc)).astype(BF16)
            qe_all = (qs * jnp.exp(bc)).astype(BF16)
            kd_all = (kh * jnp.exp(b_last - bc)).astype(BF16)
            e8 = jnp.concatenate([b_last[:, h * hdk:(h + 1) * hdk] for h in range(H_HEADS)], axis=0)
            dec_cols = jnp.exp(_transpose_rows(e8))
            operands.append((rows[j], p_all, q_all, qe_all, kd_all, dec_cols))
        yield
        yield from _round_robin([hgrn_head(grp, b, h, causal_h, operands) for h in range(H_HEADS)])

    def hgrn_head(grp, b, h, causal_h, operands):
        sl = slice(h * hdk, (h + 1) * hdk)
        n_sub = len(operands)
        state = grp.s_in[b, h]
        v_b, pq, inter, o, ms = {}, {}, {}, {}, {}
        for k in range(n_sub + 3):
            if k == 0:
                for j, (rows_j, p_all, q_all, _, _, _) in enumerate(operands):
                    v_b[j] = ih_s[rows_j, sl].astype(BF16)
                    pq[j] = _dot_nt(p_all[:, sl], q_all[:, sl])
            if k < n_sub:
                _, _, _, qe_all, kd_all, dec_cols = operands[k]
                inter[k] = _dot(qe_all[:, sl], state.astype(BF16))
                state = dec_cols[:, h:h + 1] * state + _dot_tn(kd_all[:, sl], v_b[k])
                if k == n_sub - 1:
                    grp.s_out[b, h] = state
            if 0 <= k - 1 < n_sub:
                a_mat = jnp.where(causal_h, pq[k - 1], 0.0)
                o[k - 1] = _dot(a_mat.astype(BF16), v_b[k - 1]) + inter[k - 1]
            if 0 <= k - 2 < n_sub:
                ms[k - 2] = jnp.mean(o[k - 2] * o[k - 2], axis=1, keepdims=True)
            if 0 <= k - 3 < n_sub:
                oh_s[operands[k - 3][0], sl] = (o[k - 3] * lax.rsqrt(ms[k - 3] + EPS)
                                                * h_norm_ref[:, sl])
            if k < n_sub + 2:
                yield

    gate_acts = ((1, _silu), (2, _sigmoid), (6, _silu))
    pieces = [(slot, sec, act, c) for slot, (sec, act) in enumerate(gate_acts)
              for c in range(d_m // MXU_COLS)]
    assert sum(GATE_PIECES_PER_ROUND) == len(pieces)

    def gate_pieces():
        todo = iter(pieces)
        for n in GATE_PIECES_PER_ROUND:
            for _ in range(n):
                slot, sec, act, c = next(todo)
                col = sec * d_m + c * MXU_COLS
                gact_s[slot, :, c * MXU_COLS:(c + 1) * MXU_COLS] = act(
                    _dot(xn_s[...], w_in_ref[:, col:col + MXU_COLS]))
            yield

    units = [gate_pieces()]
    for grp in groups:
        causal, causal_t = masks(grp.tc)
        causal_h, _ = masks(grp.lh)
        units += [mlstm_seq(grp, b, causal, causal_t) for b in range(grp.n_seq)]
        units += [hgrn_seq(grp, b, causal_h) for b in range(grp.n_seq)]
    for _ in _round_robin(units):
        pass

    hm = hm_s[...] * gact_s[1]
    hm = (hm + m_skip_ref[...] * xc_s[...]) * gact_s[0]
    oh = oh_s[...] * gact_s[2]
    y = (jnp.concatenate([xp_ref[...].reshape(rows_p, d_model),
                          xs_ref[...].reshape(rows_s, d_model)], axis=0)
         + _dot(hm.astype(BF16), w_out_ref[0:d_m, :])
         + _dot(oh.astype(BF16), w_out_ref[d_m:, :]))
    y = y * lax.rsqrt(jnp.mean(y * y, -1, keepdims=True) + EPS) * g_final_ref[...]
    yp_ref[...] = y[0:rows_p].reshape(n_p, tc_p, d_model)
    ys_ref[...] = y[rows_p:].reshape(n_s, tc_s, d_model)


def _const_spec(shape):
    zeros = (0,) * len(shape)
    return pl.BlockSpec(shape, lambda *_: zeros, pipeline_mode=pl.Buffered(1))


def _state_shapes(batch, d_m, dk, dv, hdk):
    return [
        jax.ShapeDtypeStruct((batch, CONV_W - 1, d_m), F32),
        jax.ShapeDtypeStruct((batch, M_HEADS, dk, dv), F32),
        jax.ShapeDtypeStruct((batch, M_HEADS, dk), F32),
        jax.ShapeDtypeStruct((batch, M_HEADS, LANES), F32),
        jax.ShapeDtypeStruct((batch, H_HEADS, hdk, hdk), F32),
    ]


def _state_specs(nb, d_m, dk, dv, hdk, batch_index):
    def spec(shape):
        zeros = (0,) * (len(shape) - 1)
        return pl.BlockSpec(shape, lambda g, t: (batch_index(g, t),) + zeros)
    return [spec((nb, CONV_W - 1, d_m)), spec((nb, M_HEADS, dk, dv)), spec((nb, M_HEADS, dk)),
            spec((nb, M_HEADS, LANES)), spec((nb, H_HEADS, hdk, hdk))]


def _mixer_call(x_p, x_s, state_s, weights):
    b_p, t_p, d_model = x_p.shape
    b_s, tc_s, _ = x_s.shape
    (g_norm, w_in, conv_w, conv_b, w_qk, w_v, w_gate, b_gate, m_ln, m_skip, lb_param,
     h_norm, w_out, g_final) = weights
    d_m = conv_w.shape[-1]
    dk, dv = w_qk.shape[-1] // 2, w_v.shape[-1]
    hdk = d_m // H_HEADS
    n_p, tc_p = PROMPT_GROUP, PROMPT_CHUNK
    n_groups, n_steps = b_p // n_p, t_p // tc_p
    n_s = b_s // (n_groups * n_steps)
    assert n_groups * n_p == b_p and n_steps * tc_p == t_p and n_s * n_groups * n_steps == b_s
    rows = n_p * tc_p + n_s * tc_s

    xp_spec = pl.BlockSpec((n_p, tc_p, d_model), lambda g, t: (g, t, 0))
    xs_spec = pl.BlockSpec((n_s, tc_s, d_model), lambda g, t: (g * n_steps + t, 0, 0))
    p_state = _state_specs(n_p, d_m, dk, dv, hdk, lambda g, t: g)
    s_state = _state_specs(n_s, d_m, dk, dv, hdk, lambda g, t: g * n_steps + t)

    scratch = [
        pltpu.VMEM((n_p, tc_p + HIST, d_m), F32),
        pltpu.VMEM((n_s, tc_s + HIST, d_m), F32),
        pltpu.VMEM((rows, d_model), BF16),
        pltpu.VMEM((rows, d_m), F32),
        pltpu.VMEM((rows, M_HEADS * dk), F32),
        pltpu.VMEM((rows, M_HEADS * dk), F32),
        pltpu.VMEM((rows, d_m), F32),
        pltpu.VMEM((rows, LANES), F32),
        pltpu.VMEM((rows, d_m), F32),
        pltpu.VMEM((rows, d_m), F32),
        pltpu.VMEM((rows, d_m), F32),
        pltpu.VMEM((rows, d_m), F32),
        pltpu.VMEM((rows, d_m), F32),
        pltpu.VMEM((rows, d_m), F32),
        pltpu.VMEM((3, rows, d_m), F32),
    ]
    return pl.pallas_call(
        functools.partial(_mixer_kernel, n_p, tc_p, n_s, tc_s),
        grid=(n_groups, n_steps),
        in_specs=[xp_spec, xs_spec] + s_state + [_const_spec(w.shape) for w in weights],
        out_specs=[xp_spec, xs_spec] + p_state + s_state,
        out_shape=([jax.ShapeDtypeStruct(x_p.shape, x_p.dtype),
                    jax.ShapeDtypeStruct(x_s.shape, x_s.dtype)]
                   + _state_shapes(b_p, d_m, dk, dv, hdk) + _state_shapes(b_s, d_m, dk, dv, hdk)),
        scratch_shapes=scratch,
        compiler_params=pltpu.CompilerParams(dimension_semantics=("arbitrary", "arbitrary"),
                                             vmem_limit_bytes=VMEM_LIMIT_BYTES),
        name="mixer_step",
    )(x_p, x_s, *state_s, *weights)


def kernel(x_prompt, x_sample, state_mlstm_conv, state_mlstm_C, state_mlstm_n, state_mlstm_m,
           state_hgrn_S, g_norm, w_in, conv_w, conv_b, w_q, w_k, w_v, w_gate, b_gate, m_ln,
           m_skip, lb_param, h_norm, w_out, g_final):
    depth = w_in.shape[0]
    assert depth == 1 and lb_param.shape[0] == 2
    d_m = conv_w.shape[-1]
    n_gate = w_gate.shape[-1]
    weights = (
        g_norm[0][None, :],
        w_in[0].astype(BF16),
        conv_w[0],
        conv_b[0][None, :],
        jnp.concatenate([w_q[0], w_k[0]], axis=-1).astype(BF16),
        w_v[0].astype(BF16),
        jnp.pad(w_gate[0], ((0, 0), (0, LANES - n_gate))).astype(BF16),
        jnp.pad(b_gate[0], (0, LANES - n_gate))[None, :],
        m_ln[0].reshape(1, d_m),
        m_skip[0][None, :],
        lb_param,
        h_norm[0].reshape(1, d_m),
        w_out[0].astype(BF16),
        g_final[None, :],
    )
    m_lanes = jnp.broadcast_to(state_mlstm_m[0][:, :, None], state_mlstm_m.shape[1:] + (LANES,))
    state = (state_mlstm_conv[0], state_mlstm_C[0], state_mlstm_n[0], m_lanes, state_hgrn_S[0])
    (y_p, y_s, p_conv, p_c, p_n, p_m, p_s, s_conv, s_c, s_n, s_m, s_s) = _mixer_call(
        x_prompt, x_sample, state, weights)
    return (y_p, y_s, p_conv[None], p_c[None], p_n[None], p_m[None, :, :, 0], p_s[None],
            s_conv[None], s_c[None], s_n[None], s_m[None, :, :, 0], s_s[None])
```

```python
import collections
import functools

import jax
import jax.numpy as jnp
from jax import lax
from jax.experimental import pallas as pl
from jax.experimental.pallas import tpu as pltpu

F32 = jnp.float32
BF16 = jnp.bfloat16

EPS = 1e-6
CONV_W = 4
M_HEADS = 4
H_HEADS = 8
LANES = 128
SUBLANES = 8
MXU_COLS = 256
HIST = SUBLANES
PROMPT_CHUNK = 64
PROMPT_GROUP = 4
HGRN_SUBCHUNK = 32
VMEM_LIMIT_BYTES = 60 * 1024 * 1024
GATE_PIECES_PER_ROUND = (1, 2, 2, 2, 2, 2, 1)

_Group = collections.namedtuple(
    "_Group", "row0 n_seq tc lh convbuf c_in n_in m_in s_in c_out n_out m_out s_out")


def _sigmoid(x):
    return 1.0 / (1.0 + jnp.exp(-x))


def _silu(x):
    return x * _sigmoid(x)


def _log_sigmoid(x):
    return jnp.minimum(x, 0.0) - jnp.log(1.0 + jnp.exp(-jnp.abs(x)))


def _dot(a, b):
    return jnp.dot(a, b, preferred_element_type=F32)


def _dot_nt(a, b):
    return lax.dot_general(a, b, (((1,), (1,)), ((), ())), preferred_element_type=F32)


def _dot_tn(a, b):
    return lax.dot_general(a, b, (((0,), (0,)), ((), ())), preferred_element_type=F32)


def _transpose_rows(a):
    r = a.shape[0]
    if r < LANES:
        a = jnp.concatenate([a, jnp.zeros((LANES - r, LANES), a.dtype)], axis=0)
    return a.T


def _round_robin(gens):
    gens = list(gens)
    while gens:
        for g in list(gens):
            try:
                next(g)
            except StopIteration:
                gens.remove(g)
        yield


def _mixer_kernel(n_p, tc_p, n_s, tc_s, *refs):
    (xp_ref, xs_ref, conv0_ref, c0_ref, n0_ref, m0_ref, s0_ref,
     g_norm_ref, w_in_ref, conv_w_ref, conv_b_ref, w_q_ref, w_k_ref, w_v_ref, w_gate_ref,
     b_gate_ref, m_ln_ref, m_skip_ref, lb_param_ref, h_norm_ref, w_out_ref, g_final_ref,
     yp_ref, ys_ref, pconv_ref, pc_ref, pn_ref, pm_ref, ps_ref,
     sconv_ref, sc_ref, sn_ref, sm_ref, ss_ref,
     convbuf_p, convbuf_s, xn_s, xc_s, q_s, k_s, v_s, g_s, hm_s, logf_s, kh_s, qs_s, ih_s, oh_s,
     gact_s) = refs

    rows_p = n_p * tc_p
    rows_s = n_s * tc_s
    d_model = xp_ref.shape[-1]
    d_m = convbuf_p.shape[-1]
    dk = w_q_ref.shape[-1]
    dv = w_v_ref.shape[-1]
    hdk = d_m // H_HEADS
    q_scale = dk ** -0.5

    prompt = _Group(0, n_p, tc_p, min(tc_p, HGRN_SUBCHUNK), convbuf_p,
                    pc_ref, pn_ref, pm_ref, ps_ref, pc_ref, pn_ref, pm_ref, ps_ref)
    sample = _Group(rows_p, n_s, tc_s, min(tc_s, HGRN_SUBCHUNK), convbuf_s,
                    c0_ref, n0_ref, m0_ref, s0_ref, sc_ref, sn_ref, sm_ref, ss_ref)
    groups = (prompt, sample)

    t_idx = pl.program_id(1)

    @pl.when(t_idx == 0)
    def _():
        pc_ref[...] = jnp.zeros_like(pc_ref)
        pn_ref[...] = jnp.zeros_like(pn_ref)
        pm_ref[...] = jnp.zeros_like(pm_ref)
        ps_ref[...] = jnp.zeros_like(ps_ref)
        convbuf_p[:, 0:HIST, :] = jnp.zeros((n_p, HIST, d_m), F32)

    @pl.when(t_idx > 0)
    def _():
        convbuf_p[:, 0:HIST, :] = convbuf_p[:, tc_p:tc_p + HIST, :]

    convbuf_s[:, HIST - (CONV_W - 1):HIST, :] = conv0_ref[...]

    x = jnp.concatenate([xp_ref[...].reshape(rows_p, d_model),
                         xs_ref[...].reshape(rows_s, d_model)], axis=0)
    xn = (x * lax.rsqrt(jnp.mean(x * x, -1, keepdims=True) + EPS) * g_norm_ref[...]).astype(BF16)
    xn_s[...] = xn

    def proj(i):
        return _dot(xn, w_in_ref[:, i * d_m:(i + 1) * d_m])

    xm = proj(0)
    for grp, conv_o_ref in ((prompt, pconv_ref), (sample, sconv_ref)):
        n_rows = grp.n_seq * grp.tc
        grp.convbuf[:, HIST:HIST + grp.tc, :] = (
            xm[grp.row0:grp.row0 + n_rows].reshape(grp.n_seq, grp.tc, d_m))
        acc = conv_b_ref[...].reshape(1, 1, d_m)
        for j in range(CONV_W):
            lo = HIST - (CONV_W - 1) + j
            acc = acc + grp.convbuf[:, lo:lo + grp.tc, :] * conv_w_ref[j:j + 1, :].reshape(1, 1, d_m)
        conv_o_ref[...] = grp.convbuf[:, grp.tc + HIST - (CONV_W - 1):grp.tc + HIST, :]
        xc_s[grp.row0:grp.row0 + n_rows, :] = _silu(acc).reshape(n_rows, d_m)

    xm_b = xm.astype(BF16)
    xc_b = xc_s[...].astype(BF16)
    for h in range(M_HEADS):
        xc_h = xc_b[:, h * dv:(h + 1) * dv]
        q_s[:, h * dk:(h + 1) * dk] = _dot(xc_h, w_q_ref[h])
        k_s[:, h * dk:(h + 1) * dk] = _dot(xc_h, w_k_ref[h])
        v_s[:, h * dv:(h + 1) * dv] = _dot(xm_b[:, h * dv:(h + 1) * dv], w_v_ref[h])
    n_qk = M_HEADS * dk
    g_s[...] = (b_gate_ref[...]
                + _dot(q_s[...].astype(BF16), w_gate_ref[0:n_qk, :])
                + _dot(k_s[...].astype(BF16), w_gate_ref[n_qk:2 * n_qk, :])
                + _dot(v_s[...].astype(BF16), w_gate_ref[2 * n_qk:, :]))

    lbp = lb_param_ref[...]
    lbe = jnp.exp(lbp - jnp.max(lbp, axis=0, keepdims=True))
    lb = lbe[0:1, :] / jnp.sum(lbe, axis=0, keepdims=True)
    sg = _sigmoid(proj(3))
    logf_s[...] = jnp.log(lb + (1.0 - lb) * sg)
    kh_s[...] = (1.0 - lb) * (1.0 - sg)
    qs_s[...] = _silu(proj(4))
    ih_s[...] = proj(5)

    neg_inf = F32(-jnp.inf)

    def masks(n):
        row = lax.broadcasted_iota(jnp.int32, (n, n), 0)
        col = lax.broadcasted_iota(jnp.int32, (n, n), 1)
        return col <= row, row <= col

    def mlstm_seq(grp, b, causal, causal_t):
        rows = pl.ds(grp.row0 + b * grp.tc, grp.tc)
        gb = g_s[rows, :]
        gt = _transpose_rows(gb)
        lf_cols = _log_sigmoid(gb)
        lf_rows = _log_sigmoid(gt[M_HEADS:2 * M_HEADS, 0:grp.tc])
        yield
        yield from _round_robin([mlstm_head(grp, b, h, rows, causal, causal_t, gb, gt, lf_cols, lf_rows)
                                 for h in range(M_HEADS)])

    def mlstm_head(grp, b, h, rows, causal, causal_t, gb, gt, lf_cols, lf_rows):
        tc = grp.tc
        i_row = gt[h:h + 1, 0:tc]
        i_col = gb[:, h:h + 1]
        lf_row = lf_rows[h:h + 1, :]
        lf_col = lf_cols[:, M_HEADS + h:M_HEADS + h + 1]
        b_col = jnp.sum(jnp.where(causal, lf_row, 0.0), axis=1, keepdims=True)
        b_row = jnp.sum(jnp.where(causal_t, lf_col, 0.0), axis=0, keepdims=True)
        q = q_s[rows, h * dk:(h + 1) * dk] * q_scale
        k = k_s[rows, h * dk:(h + 1) * dk]
        v_b = v_s[rows, h * dv:(h + 1) * dv].astype(BF16)
        q_b = q.astype(BF16)
        qk = _dot_nt(q_b, k.astype(BF16))
        c_prev = grp.c_in[b, h]
        n_prev = grp.n_in[b, h:h + 1, :]
        qc = _dot(q_b, c_prev.astype(BF16))
        qn = jnp.sum(q * n_prev, axis=1, keepdims=True)
        yield
        m_prev = grp.m_in[b, h:h + 1, 0:1]
        a_row = i_row - b_row
        a_col = i_col - b_col
        log_d = jnp.where(causal, a_row, neg_inf)
        m_col = jnp.maximum(jnp.max(log_d, axis=1, keepdims=True), m_prev)
        yield
        d_mat = jnp.exp(log_d - m_col)
        g_col = jnp.exp(m_prev - m_col)
        s_mat = qk * d_mat
        num = _dot(s_mat.astype(BF16), v_b) + g_col * qc
        den = jnp.sum(s_mat, axis=1, keepdims=True) + g_col * qn
        m_last = m_col[tc - 1:tc, :]
        decay = jnp.exp(m_prev - m_last)
        kw = k * jnp.exp(a_col - m_last)
        grp.c_out[b, h] = decay * c_prev + _dot_tn(kw.astype(BF16), v_b)
        grp.n_out[b, h:h + 1, :] = decay * n_prev + jnp.sum(kw, axis=0, keepdims=True)
        grp.m_out[b, h:h + 1, :] = jnp.broadcast_to(b_col[tc - 1:tc, :] + m_last, (1, LANES))
        yield
        m_t = b_col + m_col
        hval = num * (1.0 / jnp.maximum(jnp.abs(den), jnp.exp(-m_t)))
        mu = jnp.mean(hval, axis=1, keepdims=True)
        hc = hval - mu
        var = jnp.mean(hc * hc, axis=1, keepdims=True)
        hm_s[rows, h * dv:(h + 1) * dv] = (hc * lax.rsqrt(var + EPS)
                                           * m_ln_ref[:, h * dv:(h + 1) * dv])

    def hgrn_seq(grp, b, causal_h):
        lh = grp.lh
        n_sub = grp.tc // lh
        tril_h = causal_h.astype(F32).astype(BF16)
        rows = [pl.ds(grp.row0 + b * grp.tc + j * lh, lh) for j in range(n_sub)]
        bcs = []
        for j in range(n_sub):
            lf = logf_s[rows[j], :]
            lf_hi = lf.astype(BF16)
            lf_lo = (lf - lf_hi.astype(F32)).astype(BF16)
            bcs.append(_dot(tril_h, lf_hi) + _dot(tril_h, lf_lo))
        yield
        operands = []
        for j, bc in enumerate(bcs):
            b_mid = bc[lh // 2 - 1:lh // 2, :]
            b_last = bc[lh - 1:lh, :]
            qs = qs_s[rows[j], :]
            kh = kh_s[rows[j], :]
            p_all = (qs * jnp.exp(bc - b_mid)).astype(BF16)
            q_all = (kh * jnp.exp(b_mid - bc)).astype(BF16)
            qe_all = (qs * jnp.exp(bc)).astype(BF16)
            kd_all = (kh * jnp.exp(b_last - bc)).astype(BF16)
            e8 = jnp.concatenate([b_last[:, h * hdk:(h + 1) * hdk] for h in range(H_HEADS)], axis=0)
            dec_cols = jnp.exp(_transpose_rows(e8))
            operands.append((rows[j], p_all, q_all, qe_all, kd_all, dec_cols))
        yield
        yield from _round_robin([hgrn_head(grp, b, h, causal_h, operands) for h in range(H_HEADS)])

    def hgrn_head(grp, b, h, causal_h, operands):
        sl = slice(h * hdk, (h + 1) * hdk)
        n_sub = len(operands)
        state = grp.s_in[b, h]
        v_b, pq, inter, o, ms = {}, {}, {}, {}, {}
        for k in range(n_sub + 3):
            if k == 0:
                for j, (rows_j, p_all, q_all, _, _, _) in enumerate(operands):
                    v_b[j] = ih_s[rows_j, sl].astype(BF16)
                    pq[j] = _dot_nt(p_all[:, sl], q_all[:, sl])
            if k < n_sub:
                _, _, _, qe_all, kd_all, dec_cols = operands[k]
                inter[k] = _dot(qe_all[:, sl], state.astype(BF16))
                state = dec_cols[:, h:h + 1] * state + _dot_tn(kd_all[:, sl], v_b[k])
                if k == n_sub - 1:
                    grp.s_out[b, h] = state
            if 0 <= k - 1 < n_sub:
                a_mat = jnp.where(causal_h, pq[k - 1], 0.0)
                o[k - 1] = _dot(a_mat.astype(BF16), v_b[k - 1]) + inter[k - 1]
            if 0 <= k - 2 < n_sub:
                ms[k - 2] = jnp.mean(o[k - 2] * o[k - 2], axis=1, keepdims=True)
            if 0 <= k - 3 < n_sub:
                oh_s[operands[k - 3][0], sl] = (o[k - 3] * lax.rsqrt(ms[k - 3] + EPS)
                                                * h_norm_ref[:, sl])
            if k < n_sub + 2:
                yield

    gate_acts = ((1, _silu), (2, _sigmoid), (6, _silu))
    pieces = [(slot, sec, act, c) for slot, (sec, act) in enumerate(gate_acts)
              for c in range(d_m // MXU_COLS)]
    assert sum(GATE_PIECES_PER_ROUND) == len(pieces)

    def gate_pieces():
        todo = iter(pieces)
        for n in GATE_PIECES_PER_ROUND:
            for _ in range(n):
                slot, sec, act, c = next(todo)
                col = sec * d_m + c * MXU_COLS
                gact_s[slot, :, c * MXU_COLS:(c + 1) * MXU_COLS] = act(
                    _dot(xn_s[...], w_in_ref[:, col:col + MXU_COLS]))
            yield

    units = [gate_pieces()]
    for grp in groups:
        causal, causal_t = masks(grp.tc)
        causal_h, _ = masks(grp.lh)
        units += [mlstm_seq(grp, b, causal, causal_t) for b in range(grp.n_seq)]
        units += [hgrn_seq(grp, b, causal_h) for b in range(grp.n_seq)]
    for _ in _round_robin(units):
        pass

    hm = hm_s[...] * gact_s[1]
    hm = (hm + m_skip_ref[...] * xc_s[...]) * gact_s[0]
    oh = oh_s[...] * gact_s[2]
    y = (jnp.concatenate([xp_ref[...].reshape(rows_p, d_model),
                          xs_ref[...].reshape(rows_s, d_model)], axis=0)
         + _dot(hm.astype(BF16), w_out_ref[0:d_m, :])
         + _dot(oh.astype(BF16), w_out_ref[d_m:, :]))
    y = y * lax.rsqrt(jnp.mean(y * y, -1, keepdims=True) + EPS) * g_final_ref[...]
    yp_ref[...] = y[0:rows_p].reshape(n_p, tc_p, d_model)
    ys_ref[...] = y[rows_p:].reshape(n_s, tc_s, d_model)


def _const_spec(shape):
    zeros = (0,) * len(shape)
    return pl.BlockSpec(shape, lambda *_: zeros, pipeline_mode=pl.Buffered(1))


def _state_shapes(batch, d_m, dk, dv, hdk):
    return [
        jax.ShapeDtypeStruct((batch, CONV_W - 1, d_m), F32),
        jax.ShapeDtypeStruct((batch, M_HEADS, dk, dv), F32),
        jax.ShapeDtypeStruct((batch, M_HEADS, dk), F32),
        jax.ShapeDtypeStruct((batch, M_HEADS, LANES), F32),
        jax.ShapeDtypeStruct((batch, H_HEADS, hdk, hdk), F32),
    ]


def _state_specs(nb, d_m, dk, dv, hdk, batch_index):
    def spec(shape):
        zeros = (0,) * (len(shape) - 1)
        return pl.BlockSpec(shape, lambda g, t: (batch_index(g, t),) + zeros)
    return [spec((nb, CONV_W - 1, d_m)), spec((nb, M_HEADS, dk, dv)), spec((nb, M_HEADS, dk)),
            spec((nb, M_HEADS, LANES)), spec((nb, H_HEADS, hdk, hdk))]


def _mixer_call(x_p, x_s, state_s, weights):
    b_p, t_p, d_model = x_p.shape
    b_s, tc_s, _ = x_s.shape
    (g_norm, w_in, conv_w, conv_b, w_q, w_k, w_v, w_gate, b_gate, m_ln, m_skip, lb_param,
     h_norm, w_out, g_final) = weights
    d_m = conv_w.shape[-1]
    dk, dv = w_q.shape[-1], w_v.shape[-1]
    hdk = d_m // H_HEADS
    n_p, tc_p = PROMPT_GROUP, PROMPT_CHUNK
    n_groups, n_steps = b_p // n_p, t_p // tc_p
    n_s = b_s // (n_groups * n_steps)
    assert n_groups * n_p == b_p and n_steps * tc_p == t_p and n_s * n_groups * n_steps == b_s
    rows = n_p * tc_p + n_s * tc_s

    xp_spec = pl.BlockSpec((n_p, tc_p, d_model), lambda g, t: (g, t, 0))
    xs_spec = pl.BlockSpec((n_s, tc_s, d_model), lambda g, t: (g * n_steps + t, 0, 0))
    p_state = _state_specs(n_p, d_m, dk, dv, hdk, lambda g, t: g)
    s_state = _state_specs(n_s, d_m, dk, dv, hdk, lambda g, t: g * n_steps + t)

    scratch = [
        pltpu.VMEM((n_p, tc_p + HIST, d_m), F32),
        pltpu.VMEM((n_s, tc_s + HIST, d_m), F32),
        pltpu.VMEM((rows, d_model), BF16),
        pltpu.VMEM((rows, d_m), F32),
        pltpu.VMEM((rows, M_HEADS * dk), F32),
        pltpu.VMEM((rows, M_HEADS * dk), F32),
        pltpu.VMEM((rows, d_m), F32),
        pltpu.VMEM((rows, LANES), F32),
        pltpu.VMEM((rows, d_m), F32),
        pltpu.VMEM((rows, d_m), F32),
        pltpu.VMEM((rows, d_m), F32),
        pltpu.VMEM((rows, d_m), F32),
        pltpu.VMEM((rows, d_m), F32),
        pltpu.VMEM((rows, d_m), F32),
        pltpu.VMEM((3, rows, d_m), F32),
    ]
    return pl.pallas_call(
        functools.partial(_mixer_kernel, n_p, tc_p, n_s, tc_s),
        grid=(n_groups, n_steps),
        in_specs=[xp_spec, xs_spec] + s_state + [_const_spec(w.shape) for w in weights],
        out_specs=[xp_spec, xs_spec] + p_state + s_state,
        out_shape=([jax.ShapeDtypeStruct(x_p.shape, x_p.dtype),
                    jax.ShapeDtypeStruct(x_s.shape, x_s.dtype)]
                   + _state_shapes(b_p, d_m, dk, dv, hdk) + _state_shapes(b_s, d_m, dk, dv, hdk)),
        scratch_shapes=scratch,
        compiler_params=pltpu.CompilerParams(dimension_semantics=("arbitrary", "arbitrary"),
                                             vmem_limit_bytes=VMEM_LIMIT_BYTES),
        name="mixer_step",
    )(x_p, x_s, *state_s, *weights)


def kernel(x_prompt, x_sample, state_mlstm_conv, state_mlstm_C, state_mlstm_n, state_mlstm_m,
           state_hgrn_S, g_norm, w_in, conv_w, conv_b, w_q, w_k, w_v, w_gate, b_gate, m_ln,
           m_skip, lb_param, h_norm, w_out, g_final):
    depth = w_in.shape[0]
    assert depth == 1 and lb_param.shape[0] == 2
    d_m = conv_w.shape[-1]
    n_gate = w_gate.shape[-1]
    weights = (
        g_norm[0][None, :],
        w_in[0].astype(BF16),
        conv_w[0],
        conv_b[0][None, :],
        w_q[0].astype(BF16),
        w_k[0].astype(BF16),
        w_v[0].astype(BF16),
        jnp.pad(w_gate[0], ((0, 0), (0, LANES - n_gate))).astype(BF16),
        jnp.pad(b_gate[0], (0, LANES - n_gate))[None, :],
        m_ln[0].reshape(1, d_m),
        m_skip[0][None, :],
        lb_param,
        h_norm[0].reshape(1, d_m),
        w_out[0].astype(BF16),
        g_final[None, :],
    )
    m_lanes = jnp.broadcast_to(state_mlstm_m[0][:, :, None], state_mlstm_m.shape[1:] + (LANES,))
    state = (state_mlstm_conv[0], state_mlstm_C[0], state_mlstm_n[0], m_lanes, state_hgrn_S[0])
    (y_p, y_s, p_conv, p_c, p_n, p_m, p_s, s_conv, s_c, s_n, s_m, s_s) = _mixer_call(
        x_prompt, x_sample, state, weights)
    return (y_p, y_s, p_conv[None], p_c[None], p_n[None], p_m[None, :, :, 0], p_s[None],
            s_conv[None], s_c[None], s_n[None], s_m[None, :, :, 0], s_s[None])
```

```python
import collections
import functools

import jax
import jax.numpy as jnp
from jax import lax
from jax.experimental import pallas as pl
from jax.experimental.pallas import tpu as pltpu

F32 = jnp.float32
BF16 = jnp.bfloat16

EPS = 1e-6
CONV_W = 4
M_HEADS = 4
H_HEADS = 8
LANES = 128
SUBLANES = 8
MXU_COLS = 256
HIST = SUBLANES
PROMPT_CHUNK = 64
PROMPT_GROUP = 4
HGRN_SUBCHUNK = 32
VMEM_LIMIT_BYTES = 60 * 1024 * 1024
GATE_PIECES_PER_ROUND = (1, 2, 2, 2, 2, 2, 1)

_Group = collections.namedtuple(
    "_Group", "row0 n_seq tc lh convbuf c_in n_in m_in s_in c_out n_out m_out s_out")


def _sigmoid(x):
    return 0.5 * jnp.tanh(0.5 * x) + 0.5


def _silu(x):
    return x * _sigmoid(x)


def _log_sigmoid(x):
    return jnp.minimum(x, 0.0) - jnp.log(1.0 + jnp.exp(-jnp.abs(x)))


def _dot(a, b):
    return jnp.dot(a, b, preferred_element_type=F32)


def _dot_nt(a, b):
    return lax.dot_general(a, b, (((1,), (1,)), ((), ())), preferred_element_type=F32)


def _dot_tn(a, b):
    return lax.dot_general(a, b, (((0,), (0,)), ((), ())), preferred_element_type=F32)


def _transpose_rows(a):
    r = a.shape[0]
    if r < LANES:
        a = jnp.concatenate([a, jnp.zeros((LANES - r, LANES), a.dtype)], axis=0)
    return a.T


def _round_robin(gens):
    gens = list(gens)
    while gens:
        for g in list(gens):
            try:
                next(g)
            except StopIteration:
                gens.remove(g)
        yield


def _mixer_kernel(n_p, tc_p, n_s, tc_s, *refs):
    (xp_ref, xs_ref, conv0_ref, c0_ref, n0_ref, m0_ref, s0_ref,
     g_norm_ref, w_in_ref, conv_w_ref, conv_b_ref, w_q_ref, w_k_ref, w_v_ref, w_gate_ref,
     b_gate_ref, m_ln_ref, m_skip_ref, lb_param_ref, h_norm_ref, w_out_ref, g_final_ref,
     yp_ref, ys_ref, pconv_ref, pc_ref, pn_ref, pm_ref, ps_ref,
     sconv_ref, sc_ref, sn_ref, sm_ref, ss_ref,
     convbuf_p, convbuf_s, xn_s, xc_s, q_s, k_s, v_s, g_s, hm_s, logf_s, kh_s, qs_s, ih_s, oh_s,
     gact_s) = refs

    rows_p = n_p * tc_p
    rows_s = n_s * tc_s
    d_model = xp_ref.shape[-1]
    d_m = convbuf_p.shape[-1]
    dk = w_q_ref.shape[-1]
    dv = w_v_ref.shape[-1]
    hdk = d_m // H_HEADS
    q_scale = dk ** -0.5

    prompt = _Group(0, n_p, tc_p, min(tc_p, HGRN_SUBCHUNK), convbuf_p,
                    pc_ref, pn_ref, pm_ref, ps_ref, pc_ref, pn_ref, pm_ref, ps_ref)
    sample = _Group(rows_p, n_s, tc_s, min(tc_s, HGRN_SUBCHUNK), convbuf_s,
                    c0_ref, n0_ref, m0_ref, s0_ref, sc_ref, sn_ref, sm_ref, ss_ref)
    groups = (prompt, sample)

    t_idx = pl.program_id(1)

    @pl.when(t_idx == 0)
    def _():
        pc_ref[...] = jnp.zeros_like(pc_ref)
        pn_ref[...] = jnp.zeros_like(pn_ref)
        pm_ref[...] = jnp.zeros_like(pm_ref)
        ps_ref[...] = jnp.zeros_like(ps_ref)
        convbuf_p[:, 0:HIST, :] = jnp.zeros((n_p, HIST, d_m), F32)

    @pl.when(t_idx > 0)
    def _():
        convbuf_p[:, 0:HIST, :] = convbuf_p[:, tc_p:tc_p + HIST, :]

    convbuf_s[:, HIST - (CONV_W - 1):HIST, :] = conv0_ref[...]

    x = jnp.concatenate([xp_ref[...].reshape(rows_p, d_model),
                         xs_ref[...].reshape(rows_s, d_model)], axis=0)
    xn = (x * lax.rsqrt(jnp.mean(x * x, -1, keepdims=True) + EPS) * g_norm_ref[...]).astype(BF16)
    xn_s[...] = xn

    def proj(i):
        return _dot(xn, w_in_ref[:, i * d_m:(i + 1) * d_m])

    xm = proj(0)
    for grp, conv_o_ref in ((prompt, pconv_ref), (sample, sconv_ref)):
        n_rows = grp.n_seq * grp.tc
        grp.convbuf[:, HIST:HIST + grp.tc, :] = (
            xm[grp.row0:grp.row0 + n_rows].reshape(grp.n_seq, grp.tc, d_m))
        acc = conv_b_ref[...].reshape(1, 1, d_m)
        for j in range(CONV_W):
            lo = HIST - (CONV_W - 1) + j
            acc = acc + grp.convbuf[:, lo:lo + grp.tc, :] * conv_w_ref[j:j + 1, :].reshape(1, 1, d_m)
        conv_o_ref[...] = grp.convbuf[:, grp.tc + HIST - (CONV_W - 1):grp.tc + HIST, :]
        xc_s[grp.row0:grp.row0 + n_rows, :] = _silu(acc).reshape(n_rows, d_m)

    xm_b = xm.astype(BF16)
    xc_b = xc_s[...].astype(BF16)
    for h in range(M_HEADS):
        xc_h = xc_b[:, h * dv:(h + 1) * dv]
        q_s[:, h * dk:(h + 1) * dk] = _dot(xc_h, w_q_ref[h])
        k_s[:, h * dk:(h + 1) * dk] = _dot(xc_h, w_k_ref[h])
        v_s[:, h * dv:(h + 1) * dv] = _dot(xm_b[:, h * dv:(h + 1) * dv], w_v_ref[h])
    n_qk = M_HEADS * dk
    g_s[...] = (b_gate_ref[...]
                + _dot(q_s[...].astype(BF16), w_gate_ref[0:n_qk, :])
                + _dot(k_s[...].astype(BF16), w_gate_ref[n_qk:2 * n_qk, :])
                + _dot(v_s[...].astype(BF16), w_gate_ref[2 * n_qk:, :]))

    lbp = lb_param_ref[...]
    lbe = jnp.exp(lbp - jnp.max(lbp, axis=0, keepdims=True))
    lb = lbe[0:1, :] / jnp.sum(lbe, axis=0, keepdims=True)
    sg = _sigmoid(proj(3))
    logf_s[...] = jnp.log(lb + (1.0 - lb) * sg)
    kh_s[...] = (1.0 - lb) * (1.0 - sg)
    qs_s[...] = _silu(proj(4))
    ih_s[...] = proj(5)

    neg_inf = F32(-jnp.inf)

    def masks(n):
        row = lax.broadcasted_iota(jnp.int32, (n, n), 0)
        col = lax.broadcasted_iota(jnp.int32, (n, n), 1)
        return col <= row, row <= col

    def mlstm_seq(grp, b, causal, causal_t):
        rows = pl.ds(grp.row0 + b * grp.tc, grp.tc)
        gb = g_s[rows, :]
        gt = _transpose_rows(gb)
        lf_cols = _log_sigmoid(gb)
        lf_rows = _log_sigmoid(gt[M_HEADS:2 * M_HEADS, 0:grp.tc])
        yield
        yield from _round_robin([mlstm_head(grp, b, h, rows, causal, causal_t, gb, gt, lf_cols, lf_rows)
                                 for h in range(M_HEADS)])

    def mlstm_head(grp, b, h, rows, causal, causal_t, gb, gt, lf_cols, lf_rows):
        tc = grp.tc
        i_row = gt[h:h + 1, 0:tc]
        i_col = gb[:, h:h + 1]
        lf_row = lf_rows[h:h + 1, :]
        lf_col = lf_cols[:, M_HEADS + h:M_HEADS + h + 1]
        b_col = jnp.sum(jnp.where(causal, lf_row, 0.0), axis=1, keepdims=True)
        b_row = jnp.sum(jnp.where(causal_t, lf_col, 0.0), axis=0, keepdims=True)
        q = q_s[rows, h * dk:(h + 1) * dk] * q_scale
        k = k_s[rows, h * dk:(h + 1) * dk]
        v_b = v_s[rows, h * dv:(h + 1) * dv].astype(BF16)
        q_b = q.astype(BF16)
        qk = _dot_nt(q_b, k.astype(BF16))
        c_prev = grp.c_in[b, h]
        n_prev = grp.n_in[b, h:h + 1, :]
        qc = _dot(q_b, c_prev.astype(BF16))
        qn = jnp.sum(q * n_prev, axis=1, keepdims=True)
        yield
        m_prev = grp.m_in[b, h:h + 1, 0:1]
        a_row = i_row - b_row
        a_col = i_col - b_col
        log_d = jnp.where(causal, a_row, neg_inf)
        m_col = jnp.maximum(jnp.max(log_d, axis=1, keepdims=True), m_prev)
        yield
        d_mat = jnp.exp(log_d - m_col)
        g_col = jnp.exp(m_prev - m_col)
        s_mat = qk * d_mat
        num = _dot(s_mat.astype(BF16), v_b) + g_col * qc
        den = jnp.sum(s_mat, axis=1, keepdims=True) + g_col * qn
        m_last = m_col[tc - 1:tc, :]
        decay = jnp.exp(m_prev - m_last)
        kw = k * jnp.exp(a_col - m_last)
        grp.c_out[b, h] = decay * grp.c_in[b, h] + _dot_tn(kw.astype(BF16), v_b)
        grp.n_out[b, h:h + 1, :] = decay * n_prev + jnp.sum(kw, axis=0, keepdims=True)
        grp.m_out[b, h:h + 1, :] = jnp.broadcast_to(b_col[tc - 1:tc, :] + m_last, (1, LANES))
        yield
        m_t = b_col + m_col
        hval = num * (1.0 / jnp.maximum(jnp.abs(den), jnp.exp(-m_t)))
        mu = jnp.mean(hval, axis=1, keepdims=True)
        hc = hval - mu
        var = jnp.mean(hc * hc, axis=1, keepdims=True)
        hm_s[rows, h * dv:(h + 1) * dv] = (hc * lax.rsqrt(var + EPS)
                                           * m_ln_ref[:, h * dv:(h + 1) * dv])

    def hgrn_seq(grp, b, causal_h):
        lh = grp.lh
        n_sub = grp.tc // lh
        tril_h = causal_h.astype(F32).astype(BF16)
        rows = [pl.ds(grp.row0 + b * grp.tc + j * lh, lh) for j in range(n_sub)]
        bcs = []
        for j in range(n_sub):
            lf = logf_s[rows[j], :]
            lf_hi = lf.astype(BF16)
            lf_lo = (lf - lf_hi.astype(F32)).astype(BF16)
            bcs.append(_dot(tril_h, lf_hi) + _dot(tril_h, lf_lo))
        yield
        operands = []
        for j, bc in enumerate(bcs):
            b_mid = bc[lh // 2 - 1:lh // 2, :]
            b_last = bc[lh - 1:lh, :]
            qs = qs_s[rows[j], :]
            kh = kh_s[rows[j], :]
            p_all = (qs * jnp.exp(bc - b_mid)).astype(BF16)
            q_all = (kh * jnp.exp(b_mid - bc)).astype(BF16)
            qe_all = (qs * jnp.exp(bc)).astype(BF16)
            kd_all = (kh * jnp.exp(b_last - bc)).astype(BF16)
            e8 = jnp.concatenate([b_last[:, h * hdk:(h + 1) * hdk] for h in range(H_HEADS)], axis=0)
            dec_cols = jnp.exp(_transpose_rows(e8))
            operands.append((rows[j], p_all, q_all, qe_all, kd_all, dec_cols))
        yield
        yield from _round_robin([hgrn_head(grp, b, h, causal_h, operands) for h in range(H_HEADS)])

    def hgrn_head(grp, b, h, causal_h, operands):
        sl = slice(h * hdk, (h + 1) * hdk)
        n_sub = len(operands)
        state = grp.s_in[b, h]
        v_b, pq, inter, o, ms = {}, {}, {}, {}, {}
        for k in range(n_sub + 3):
            if k == 0:
                for j, (rows_j, p_all, q_all, _, _, _) in enumerate(operands):
                    v_b[j] = ih_s[rows_j, sl].astype(BF16)
                    pq[j] = _dot_nt(p_all[:, sl], q_all[:, sl])
            if k < n_sub:
                _, _, _, qe_all, kd_all, dec_cols = operands[k]
                inter[k] = _dot(qe_all[:, sl], state.astype(BF16))
                state = dec_cols[:, h:h + 1] * state + _dot_tn(kd_all[:, sl], v_b[k])
                if k == n_sub - 1:
                    grp.s_out[b, h] = state
            if 0 <= k - 1 < n_sub:
                a_mat = jnp.where(causal_h, pq[k - 1], 0.0)
                o[k - 1] = _dot(a_mat.astype(BF16), v_b[k - 1]) + inter[k - 1]
            if 0 <= k - 2 < n_sub:
                ms[k - 2] = jnp.mean(o[k - 2] * o[k - 2], axis=1, keepdims=True)
            if 0 <= k - 3 < n_sub:
                oh_s[operands[k - 3][0], sl] = (o[k - 3] * lax.rsqrt(ms[k - 3] + EPS)
                                                * h_norm_ref[:, sl])
            if k < n_sub + 2:
                yield

    gate_acts = ((1, _silu), (2, _sigmoid), (6, _silu))
    pieces = [(slot, sec, act, c) for slot, (sec, act) in enumerate(gate_acts)
              for c in range(d_m // MXU_COLS)]
    assert sum(GATE_PIECES_PER_ROUND) == len(pieces)

    def gate_pieces():
        todo = iter(pieces)
        for n in GATE_PIECES_PER_ROUND:
            for _ in range(n):
                slot, sec, act, c = next(todo)
                col = sec * d_m + c * MXU_COLS
                gact_s[slot, :, c * MXU_COLS:(c + 1) * MXU_COLS] = act(
                    _dot(xn_s[...], w_in_ref[:, col:col + MXU_COLS]))
            yield

    units = [gate_pieces()]
    for grp in groups:
        causal, causal_t = masks(grp.tc)
        units += [mlstm_seq(grp, b, causal, causal_t) for b in range(grp.n_seq)]
    for grp in groups:
        causal_h, _ = masks(grp.lh)
        units += [hgrn_seq(grp, b, causal_h) for b in range(grp.n_seq)]
    for _ in _round_robin(units):
        pass

    hm = hm_s[...] * gact_s[1]
    hm = (hm + m_skip_ref[...] * xc_s[...]) * gact_s[0]
    oh = oh_s[...] * gact_s[2]
    y = (jnp.concatenate([xp_ref[...].reshape(rows_p, d_model),
                          xs_ref[...].reshape(rows_s, d_model)], axis=0)
         + _dot(hm.astype(BF16), w_out_ref[0:d_m, :])
         + _dot(oh.astype(BF16), w_out_ref[d_m:, :]))
    y = y * lax.rsqrt(jnp.mean(y * y, -1, keepdims=True) + EPS) * g_final_ref[...]
    yp_ref[...] = y[0:rows_p].reshape(n_p, tc_p, d_model)
    ys_ref[...] = y[rows_p:].reshape(n_s, tc_s, d_model)


def _const_spec(shape):
    zeros = (0,) * len(shape)
    return pl.BlockSpec(shape, lambda *_: zeros, pipeline_mode=pl.Buffered(1))


def _state_shapes(batch, d_m, dk, dv, hdk):
    return [
        jax.ShapeDtypeStruct((batch, CONV_W - 1, d_m), F32),
        jax.ShapeDtypeStruct((batch, M_HEADS, dk, dv), F32),
        jax.ShapeDtypeStruct((batch, M_HEADS, dk), F32),
        jax.ShapeDtypeStruct((batch, M_HEADS, LANES), F32),
        jax.ShapeDtypeStruct((batch, H_HEADS, hdk, hdk), F32),
    ]


def _state_specs(nb, d_m, dk, dv, hdk, batch_index):
    def spec(shape):
        zeros = (0,) * (len(shape) - 1)
        return pl.BlockSpec(shape, lambda g, t: (batch_index(g, t),) + zeros)
    return [spec((nb, CONV_W - 1, d_m)), spec((nb, M_HEADS, dk, dv)), spec((nb, M_HEADS, dk)),
            spec((nb, M_HEADS, LANES)), spec((nb, H_HEADS, hdk, hdk))]


def _mixer_call(x_p, x_s, state_s, weights):
    b_p, t_p, d_model = x_p.shape
    b_s, tc_s, _ = x_s.shape
    (g_norm, w_in, conv_w, conv_b, w_q, w_k, w_v, w_gate, b_gate, m_ln, m_skip, lb_param,
     h_norm, w_out, g_final) = weights
    d_m = conv_w.shape[-1]
    dk, dv = w_q.shape[-1], w_v.shape[-1]
    hdk = d_m // H_HEADS
    n_p, tc_p = PROMPT_GROUP, PROMPT_CHUNK
    n_groups, n_steps = b_p // n_p, t_p // tc_p
    n_s = b_s // (n_groups * n_steps)
    assert n_groups * n_p == b_p and n_steps * tc_p == t_p and n_s * n_groups * n_steps == b_s
    rows = n_p * tc_p + n_s * tc_s

    xp_spec = pl.BlockSpec((n_p, tc_p, d_model), lambda g, t: (g, t, 0))
    xs_spec = pl.BlockSpec((n_s, tc_s, d_model), lambda g, t: (g * n_steps + t, 0, 0))
    p_state = _state_specs(n_p, d_m, dk, dv, hdk, lambda g, t: g)
    s_state = _state_specs(n_s, d_m, dk, dv, hdk, lambda g, t: g * n_steps + t)

    scratch = [
        pltpu.VMEM((n_p, tc_p + HIST, d_m), F32),
        pltpu.VMEM((n_s, tc_s + HIST, d_m), F32),
        pltpu.VMEM((rows, d_model), BF16),
        pltpu.VMEM((rows, d_m), F32),
        pltpu.VMEM((rows, M_HEADS * dk), F32),
        pltpu.VMEM((rows, M_HEADS * dk), F32),
        pltpu.VMEM((rows, d_m), F32),
        pltpu.VMEM((rows, LANES), F32),
        pltpu.VMEM((rows, d_m), F32),
        pltpu.VMEM((rows, d_m), F32),
        pltpu.VMEM((rows, d_m), F32),
        pltpu.VMEM((rows, d_m), F32),
        pltpu.VMEM((rows, d_m), F32),
        pltpu.VMEM((rows, d_m), F32),
        pltpu.VMEM((3, rows, d_m), F32),
    ]
    return pl.pallas_call(
        functools.partial(_mixer_kernel, n_p, tc_p, n_s, tc_s),
        grid=(n_groups, n_steps),
        in_specs=[xp_spec, xs_spec] + s_state + [_const_spec(w.shape) for w in weights],
        out_specs=[xp_spec, xs_spec] + p_state + s_state,
        out_shape=([jax.ShapeDtypeStruct(x_p.shape, x_p.dtype),
                    jax.ShapeDtypeStruct(x_s.shape, x_s.dtype)]
                   + _state_shapes(b_p, d_m, dk, dv, hdk) + _state_shapes(b_s, d_m, dk, dv, hdk)),
        scratch_shapes=scratch,
        compiler_params=pltpu.CompilerParams(dimension_semantics=("arbitrary", "arbitrary"),
                                             vmem_limit_bytes=VMEM_LIMIT_BYTES),
        name="mixer_step",
    )(x_p, x_s, *state_s, *weights)


def kernel(x_prompt, x_sample, state_mlstm_conv, state_mlstm_C, state_mlstm_n, state_mlstm_m,
           state_hgrn_S, g_norm, w_in, conv_w, conv_b, w_q, w_k, w_v, w_gate, b_gate, m_ln,
           m_skip, lb_param, h_norm, w_out, g_final):
    depth = w_in.shape[0]
    assert depth == 1 and lb_param.shape[0] == 2
    d_m = conv_w.shape[-1]
    n_gate = w_gate.shape[-1]
    weights = (
        g_norm[0][None, :],
        w_in[0].astype(BF16),
        conv_w[0],
        conv_b[0][None, :],
        w_q[0].astype(BF16),
        w_k[0].astype(BF16),
        w_v[0].astype(BF16),
        jnp.pad(w_gate[0], ((0, 0), (0, LANES - n_gate))).astype(BF16),
        jnp.pad(b_gate[0], (0, LANES - n_gate))[None, :],
        m_ln[0].reshape(1, d_m),
        m_skip[0][None, :],
        lb_param,
        h_norm[0].reshape(1, d_m),
        w_out[0].astype(BF16),
        g_final[None, :],
    )
    m_lanes = jnp.broadcast_to(state_mlstm_m[0][:, :, None], state_mlstm_m.shape[1:] + (LANES,))
    state = (state_mlstm_conv[0], state_mlstm_C[0], state_mlstm_n[0], m_lanes, state_hgrn_S[0])
    (y_p, y_s, p_conv, p_c, p_n, p_m, p_s, s_conv, s_c, s_n, s_m, s_s) = _mixer_call(
        x_prompt, x_sample, state, weights)
    return (y_p, y_s, p_conv[None], p_c[None], p_n[None], p_m[None, :, :, 0], p_s[None],
            s_conv[None], s_c[None], s_n[None], s_m[None, :, :, 0], s_s[None])
```

```python
import collections
import functools

import jax
import jax.numpy as jnp
from jax import lax
from jax.experimental import pallas as pl
from jax.experimental.pallas import tpu as pltpu

F32 = jnp.float32
BF16 = jnp.bfloat16

EPS = 1e-6
CONV_W = 4
M_HEADS = 4
H_HEADS = 8
LANES = 128
SUBLANES = 8
MXU_COLS = 256
HIST = SUBLANES
PROMPT_CHUNK = 64
PROMPT_GROUP = 4
HGRN_SUBCHUNK = 32
VMEM_LIMIT_BYTES = 60 * 1024 * 1024
GATE_PIECES_PER_ROUND = (1, 3, 3, 1, 2, 1, 1)

_Group = collections.namedtuple(
    "_Group", "row0 n_seq tc lh convbuf c_in n_in m_in s_in c_out n_out m_out s_out")


def _sigmoid(x):
    return 0.5 * jnp.tanh(0.5 * x) + 0.5


def _silu(x):
    return x * _sigmoid(x)


def _log_sigmoid(x):
    return jnp.minimum(x, 0.0) - jnp.log(1.0 + jnp.exp(-jnp.abs(x)))


def _dot(a, b):
    return jnp.dot(a, b, preferred_element_type=F32)


def _dot_nt(a, b):
    return lax.dot_general(a, b, (((1,), (1,)), ((), ())), preferred_element_type=F32)


def _dot_tn(a, b):
    return lax.dot_general(a, b, (((0,), (0,)), ((), ())), preferred_element_type=F32)


def _transpose_rows(a):
    r = a.shape[0]
    if r < LANES:
        a = jnp.concatenate([a, jnp.zeros((LANES - r, LANES), a.dtype)], axis=0)
    return a.T


def _round_robin(gens):
    gens = list(gens)
    while gens:
        for g in list(gens):
            try:
                next(g)
            except StopIteration:
                gens.remove(g)
        yield


def _mixer_kernel(n_p, tc_p, n_s, tc_s, *refs):
    (xp_ref, xs_ref, conv0_ref, c0_ref, n0_ref, m0_ref, s0_ref,
     g_norm_ref, w_in_ref, conv_w_ref, conv_b_ref, w_q_ref, w_k_ref, w_v_ref, w_gate_ref,
     b_gate_ref, m_ln_ref, m_skip_ref, lb_param_ref, h_norm_ref, w_out_ref, g_final_ref,
     yp_ref, ys_ref, pconv_ref, pc_ref, pn_ref, pm_ref, ps_ref,
     sconv_ref, sc_ref, sn_ref, sm_ref, ss_ref,
     convbuf_p, convbuf_s, xn_s, xc_s, q_s, k_s, v_s, g_s, hm_s, logf_s, kh_s, qs_s, ih_s, oh_s,
     gact_s) = refs

    rows_p = n_p * tc_p
    rows_s = n_s * tc_s
    d_model = xp_ref.shape[-1]
    d_m = convbuf_p.shape[-1]
    dk = w_q_ref.shape[-1]
    dv = w_v_ref.shape[-1]
    hdk = d_m // H_HEADS
    q_scale = dk ** -0.5

    prompt = _Group(0, n_p, tc_p, min(tc_p, HGRN_SUBCHUNK), convbuf_p,
                    pc_ref, pn_ref, pm_ref, ps_ref, pc_ref, pn_ref, pm_ref, ps_ref)
    sample = _Group(rows_p, n_s, tc_s, min(tc_s, HGRN_SUBCHUNK), convbuf_s,
                    c0_ref, n0_ref, m0_ref, s0_ref, sc_ref, sn_ref, sm_ref, ss_ref)
    groups = (prompt, sample)

    t_idx = pl.program_id(1)

    @pl.when(t_idx == 0)
    def _():
        pc_ref[...] = jnp.zeros_like(pc_ref)
        pn_ref[...] = jnp.zeros_like(pn_ref)
        pm_ref[...] = jnp.zeros_like(pm_ref)
        ps_ref[...] = jnp.zeros_like(ps_ref)
        convbuf_p[:, 0:HIST, :] = jnp.zeros((n_p, HIST, d_m), F32)

    @pl.when(t_idx > 0)
    def _():
        convbuf_p[:, 0:HIST, :] = convbuf_p[:, tc_p:tc_p + HIST, :]

    convbuf_s[:, HIST - (CONV_W - 1):HIST, :] = conv0_ref[...]

    x = jnp.concatenate([xp_ref[...].reshape(rows_p, d_model),
                         xs_ref[...].reshape(rows_s, d_model)], axis=0)
    xn = (x * lax.rsqrt(jnp.mean(x * x, -1, keepdims=True) + EPS) * g_norm_ref[...]).astype(BF16)
    xn_s[...] = xn

    def proj(i):
        return _dot(xn, w_in_ref[:, i * d_m:(i + 1) * d_m])

    xm = proj(0)
    for grp, conv_o_ref in ((prompt, pconv_ref), (sample, sconv_ref)):
        n_rows = grp.n_seq * grp.tc
        grp.convbuf[:, HIST:HIST + grp.tc, :] = (
            xm[grp.row0:grp.row0 + n_rows].reshape(grp.n_seq, grp.tc, d_m))
        acc = conv_b_ref[...].reshape(1, 1, d_m)
        for j in range(CONV_W):
            lo = HIST - (CONV_W - 1) + j
            acc = acc + grp.convbuf[:, lo:lo + grp.tc, :] * conv_w_ref[j:j + 1, :].reshape(1, 1, d_m)
        conv_o_ref[...] = grp.convbuf[:, grp.tc + HIST - (CONV_W - 1):grp.tc + HIST, :]
        xc_s[grp.row0:grp.row0 + n_rows, :] = _silu(acc).reshape(n_rows, d_m)

    xm_b = xm.astype(BF16)
    xc_b = xc_s[...].astype(BF16)
    for h in range(M_HEADS):
        xc_h = xc_b[:, h * dv:(h + 1) * dv]
        q_s[:, h * dk:(h + 1) * dk] = _dot(xc_h, w_q_ref[h])
        k_s[:, h * dk:(h + 1) * dk] = _dot(xc_h, w_k_ref[h])
        v_s[:, h * dv:(h + 1) * dv] = _dot(xm_b[:, h * dv:(h + 1) * dv], w_v_ref[h])
    n_qk = M_HEADS * dk
    g_s[...] = (b_gate_ref[...]
                + _dot(q_s[...].astype(BF16), w_gate_ref[0:n_qk, :])
                + _dot(k_s[...].astype(BF16), w_gate_ref[n_qk:2 * n_qk, :])
                + _dot(v_s[...].astype(BF16), w_gate_ref[2 * n_qk:, :]))

    lbp = lb_param_ref[...]
    lbe = jnp.exp(lbp - jnp.max(lbp, axis=0, keepdims=True))
    lb = lbe[0:1, :] / jnp.sum(lbe, axis=0, keepdims=True)
    sg = _sigmoid(proj(3))
    logf_s[...] = jnp.log(lb + (1.0 - lb) * sg)
    kh_s[...] = (1.0 - lb) * (1.0 - sg)
    qs_s[...] = _silu(proj(4))
    ih_s[...] = proj(5)

    neg_inf = F32(-jnp.inf)

    def masks(n):
        row = lax.broadcasted_iota(jnp.int32, (n, n), 0)
        col = lax.broadcasted_iota(jnp.int32, (n, n), 1)
        return col <= row, row <= col

    def mlstm_seq(grp, b, causal, causal_t):
        rows = pl.ds(grp.row0 + b * grp.tc, grp.tc)
        gb = g_s[rows, :]
        gt = _transpose_rows(gb)
        lf_cols = _log_sigmoid(gb)
        lf_rows = _log_sigmoid(gt[M_HEADS:2 * M_HEADS, 0:grp.tc])
        yield
        yield from _round_robin([mlstm_head(grp, b, h, rows, causal, causal_t, gb, gt, lf_cols, lf_rows)
                                 for h in range(M_HEADS)])

    def mlstm_head(grp, b, h, rows, causal, causal_t, gb, gt, lf_cols, lf_rows):
        tc = grp.tc
        i_row = gt[h:h + 1, 0:tc]
        i_col = gb[:, h:h + 1]
        lf_row = lf_rows[h:h + 1, :]
        lf_col = lf_cols[:, M_HEADS + h:M_HEADS + h + 1]
        b_col = jnp.sum(jnp.where(causal, lf_row, 0.0), axis=1, keepdims=True)
        b_row = jnp.sum(jnp.where(causal_t, lf_col, 0.0), axis=0, keepdims=True)
        q = q_s[rows, h * dk:(h + 1) * dk] * q_scale
        k = k_s[rows, h * dk:(h + 1) * dk]
        v_b = v_s[rows, h * dv:(h + 1) * dv].astype(BF16)
        q_b = q.astype(BF16)
        qk = _dot_nt(q_b, k.astype(BF16))
        c_prev = grp.c_in[b, h]
        n_prev = grp.n_in[b, h:h + 1, :]
        qc = _dot(q_b, c_prev.astype(BF16))
        qn = jnp.sum(q * n_prev, axis=1, keepdims=True)
        yield
        m_prev = grp.m_in[b, h:h + 1, 0:1]
        a_row = i_row - b_row
        a_col = i_col - b_col
        log_d = jnp.where(causal, a_row, neg_inf)
        m_col = jnp.maximum(jnp.max(log_d, axis=1, keepdims=True), m_prev)
        yield
        d_mat = jnp.exp(log_d - m_col)
        g_col = jnp.exp(m_prev - m_col)
        s_mat = qk * d_mat
        num = _dot(s_mat.astype(BF16), v_b) + g_col * qc
        den = jnp.sum(s_mat, axis=1, keepdims=True) + g_col * qn
        m_last = m_col[tc - 1:tc, :]
        decay = jnp.exp(m_prev - m_last)
        kw = k * jnp.exp(a_col - m_last)
        grp.c_out[b, h] = decay * grp.c_in[b, h] + _dot_tn(kw.astype(BF16), v_b)
        grp.n_out[b, h:h + 1, :] = decay * n_prev + jnp.sum(kw, axis=0, keepdims=True)
        grp.m_out[b, h:h + 1, :] = jnp.broadcast_to(b_col[tc - 1:tc, :] + m_last, (1, LANES))
        yield
        m_t = b_col + m_col
        hval = num * (1.0 / jnp.maximum(jnp.abs(den), jnp.exp(-m_t)))
        mu = jnp.mean(hval, axis=1, keepdims=True)
        hc = hval - mu
        var = jnp.mean(hc * hc, axis=1, keepdims=True)
        hm_s[rows, h * dv:(h + 1) * dv] = (hc * lax.rsqrt(var + EPS)
                                           * m_ln_ref[:, h * dv:(h + 1) * dv])

    def hgrn_seq(grp, b, causal_h):
        lh = grp.lh
        n_sub = grp.tc // lh
        tril_h = causal_h.astype(F32).astype(BF16)
        rows = [pl.ds(grp.row0 + b * grp.tc + j * lh, lh) for j in range(n_sub)]
        bcs = []
        for j in range(n_sub):
            lf = logf_s[rows[j], :]
            lf_hi = lf.astype(BF16)
            lf_lo = (lf - lf_hi.astype(F32)).astype(BF16)
            bcs.append(_dot(tril_h, lf_hi) + _dot(tril_h, lf_lo))
        yield
        operands = []
        for j, bc in enumerate(bcs):
            b_mid = bc[lh // 2 - 1:lh // 2, :]
            b_last = bc[lh - 1:lh, :]
            qs = qs_s[rows[j], :]
            kh = kh_s[rows[j], :]
            p_all = (qs * jnp.exp(bc - b_mid)).astype(BF16)
            q_all = (kh * jnp.exp(b_mid - bc)).astype(BF16)
            qe_all = (qs * jnp.exp(bc)).astype(BF16)
            kd_all = (kh * jnp.exp(b_last - bc)).astype(BF16)
            e8 = jnp.concatenate([b_last[:, h * hdk:(h + 1) * hdk] for h in range(H_HEADS)], axis=0)
            dec_cols = jnp.exp(_transpose_rows(e8))
            operands.append((rows[j], p_all, q_all, qe_all, kd_all, dec_cols))
        yield
        yield from _round_robin([hgrn_head(grp, b, h, causal_h, operands) for h in range(H_HEADS)])

    def hgrn_head(grp, b, h, causal_h, operands):
        sl = slice(h * hdk, (h + 1) * hdk)
        n_sub = len(operands)
        state = grp.s_in[b, h]
        v_b, pq, inter, o, ms = {}, {}, {}, {}, {}
        for k in range(n_sub + 3):
            if k == 0:
                for j, (rows_j, p_all, q_all, _, _, _) in enumerate(operands):
                    v_b[j] = ih_s[rows_j, sl].astype(BF16)
                    pq[j] = _dot_nt(p_all[:, sl], q_all[:, sl])
            if k < n_sub:
                _, _, _, qe_all, kd_all, dec_cols = operands[k]
                inter[k] = _dot(qe_all[:, sl], state.astype(BF16))
                state = dec_cols[:, h:h + 1] * state + _dot_tn(kd_all[:, sl], v_b[k])
                if k == n_sub - 1:
                    grp.s_out[b, h] = state
            if 0 <= k - 1 < n_sub:
                a_mat = jnp.where(causal_h, pq[k - 1], 0.0)
                o[k - 1] = _dot(a_mat.astype(BF16), v_b[k - 1]) + inter[k - 1]
            if 0 <= k - 2 < n_sub:
                ms[k - 2] = jnp.mean(o[k - 2] * o[k - 2], axis=1, keepdims=True)
            if 0 <= k - 3 < n_sub:
                oh_s[operands[k - 3][0], sl] = (o[k - 3] * lax.rsqrt(ms[k - 3] + EPS)
                                                * h_norm_ref[:, sl])
            if k < n_sub + 2:
                yield

    gate_acts = ((1, _silu), (2, _sigmoid), (6, _silu))
    pieces = [(slot, sec, act, c) for slot, (sec, act) in enumerate(gate_acts)
              for c in range(d_m // MXU_COLS)]
    assert sum(GATE_PIECES_PER_ROUND) == len(pieces)

    def gate_pieces():
        todo = iter(pieces)
        for n in GATE_PIECES_PER_ROUND:
            for _ in range(n):
                slot, sec, act, c = next(todo)
                col = sec * d_m + c * MXU_COLS
                gact_s[slot, :, c * MXU_COLS:(c + 1) * MXU_COLS] = act(
                    _dot(xn_s[...], w_in_ref[:, col:col + MXU_COLS]))
            yield

    units = [gate_pieces()]
    for grp in groups:
        causal, causal_t = masks(grp.tc)
        units += [mlstm_seq(grp, b, causal, causal_t) for b in range(grp.n_seq)]
    for grp in groups:
        causal_h, _ = masks(grp.lh)
        units += [hgrn_seq(grp, b, causal_h) for b in range(grp.n_seq)]
    for _ in _round_robin(units):
        pass

    hm = hm_s[...] * gact_s[1]
    hm = (hm + m_skip_ref[...] * xc_s[...]) * gact_s[0]
    oh = oh_s[...] * gact_s[2]
    y = (jnp.concatenate([xp_ref[...].reshape(rows_p, d_model),
                          xs_ref[...].reshape(rows_s, d_model)], axis=0)
         + _dot(hm.astype(BF16), w_out_ref[0:d_m, :])
         + _dot(oh.astype(BF16), w_out_ref[d_m:, :]))
    y = y * lax.rsqrt(jnp.mean(y * y, -1, keepdims=True) + EPS) * g_final_ref[...]
    yp_ref[...] = y[0:rows_p].reshape(n_p, tc_p, d_model)
    ys_ref[...] = y[rows_p:].reshape(n_s, tc_s, d_model)


def _const_spec(shape):
    zeros = (0,) * len(shape)
    return pl.BlockSpec(shape, lambda *_: zeros, pipeline_mode=pl.Buffered(1))


def _state_shapes(batch, d_m, dk, dv, hdk):
    return [
        jax.ShapeDtypeStruct((batch, CONV_W - 1, d_m), F32),
        jax.ShapeDtypeStruct((batch, M_HEADS, dk, dv), F32),
        jax.ShapeDtypeStruct((batch, M_HEADS, dk), F32),
        jax.ShapeDtypeStruct((batch, M_HEADS, LANES), F32),
        jax.ShapeDtypeStruct((batch, H_HEADS, hdk, hdk), F32),
    ]


def _state_specs(nb, d_m, dk, dv, hdk, batch_index):
    def spec(shape):
        zeros = (0,) * (len(shape) - 1)
        return pl.BlockSpec(shape, lambda g, t: (batch_index(g, t),) + zeros)
    return [spec((nb, CONV_W - 1, d_m)), spec((nb, M_HEADS, dk, dv)), spec((nb, M_HEADS, dk)),
            spec((nb, M_HEADS, LANES)), spec((nb, H_HEADS, hdk, hdk))]


def _mixer_call(x_p, x_s, state_s, weights):
    b_p, t_p, d_model = x_p.shape
    b_s, tc_s, _ = x_s.shape
    (g_norm, w_in, conv_w, conv_b, w_q, w_k, w_v, w_gate, b_gate, m_ln, m_skip, lb_param,
     h_norm, w_out, g_final) = weights
    d_m = conv_w.shape[-1]
    dk, dv = w_q.shape[-1], w_v.shape[-1]
    hdk = d_m // H_HEADS
    n_p, tc_p = PROMPT_GROUP, PROMPT_CHUNK
    n_groups, n_steps = b_p // n_p, t_p // tc_p
    n_s = b_s // (n_groups * n_steps)
    assert n_groups * n_p == b_p and n_steps * tc_p == t_p and n_s * n_groups * n_steps == b_s
    rows = n_p * tc_p + n_s * tc_s

    xp_spec = pl.BlockSpec((n_p, tc_p, d_model), lambda g, t: (g, t, 0))
    xs_spec = pl.BlockSpec((n_s, tc_s, d_model), lambda g, t: (g * n_steps + t, 0, 0))
    p_state = _state_specs(n_p, d_m, dk, dv, hdk, lambda g, t: g)
    s_state = _state_specs(n_s, d_m, dk, dv, hdk, lambda g, t: g * n_steps + t)

    scratch = [
        pltpu.VMEM((n_p, tc_p + HIST, d_m), F32),
        pltpu.VMEM((n_s, tc_s + HIST, d_m), F32),
        pltpu.VMEM((rows, d_model), BF16),
        pltpu.VMEM((rows, d_m), F32),
        pltpu.VMEM((rows, M_HEADS * dk), F32),
        pltpu.VMEM((rows, M_HEADS * dk), F32),
        pltpu.VMEM((rows, d_m), F32),
        pltpu.VMEM((rows, LANES), F32),
        pltpu.VMEM((rows, d_m), F32),
        pltpu.VMEM((rows, d_m), F32),
        pltpu.VMEM((rows, d_m), F32),
        pltpu.VMEM((rows, d_m), F32),
        pltpu.VMEM((rows, d_m), F32),
        pltpu.VMEM((rows, d_m), F32),
        pltpu.VMEM((3, rows, d_m), F32),
    ]
    return pl.pallas_call(
        functools.partial(_mixer_kernel, n_p, tc_p, n_s, tc_s),
        grid=(n_groups, n_steps),
        in_specs=[xp_spec, xs_spec] + s_state + [_const_spec(w.shape) for w in weights],
        out_specs=[xp_spec, xs_spec] + p_state + s_state,
        out_shape=([jax.ShapeDtypeStruct(x_p.shape, x_p.dtype),
                    jax.ShapeDtypeStruct(x_s.shape, x_s.dtype)]
                   + _state_shapes(b_p, d_m, dk, dv, hdk) + _state_shapes(b_s, d_m, dk, dv, hdk)),
        scratch_shapes=scratch,
        compiler_params=pltpu.CompilerParams(dimension_semantics=("arbitrary", "arbitrary"),
                                             vmem_limit_bytes=VMEM_LIMIT_BYTES),
        name="mixer_step",
    )(x_p, x_s, *state_s, *weights)


def kernel(x_prompt, x_sample, state_mlstm_conv, state_mlstm_C, state_mlstm_n, state_mlstm_m,
           state_hgrn_S, g_norm, w_in, conv_w, conv_b, w_q, w_k, w_v, w_gate, b_gate, m_ln,
           m_skip, lb_param, h_norm, w_out, g_final):
    depth = w_in.shape[0]
    assert depth == 1 and lb_param.shape[0] == 2
    d_m = conv_w.shape[-1]
    n_gate = w_gate.shape[-1]
    weights = (
        g_norm[0][None, :],
        w_in[0].astype(BF16),
        conv_w[0],
        conv_b[0][None, :],
        w_q[0].astype(BF16),
        w_k[0].astype(BF16),
        w_v[0].astype(BF16),
        jnp.pad(w_gate[0], ((0, 0), (0, LANES - n_gate))).astype(BF16),
        jnp.pad(b_gate[0], (0, LANES - n_gate))[None, :],
        m_ln[0].reshape(1, d_m),
        m_skip[0][None, :],
        lb_param,
        h_norm[0].reshape(1, d_m),
        w_out[0].astype(BF16),
        g_final[None, :],
    )
    m_lanes = jnp.broadcast_to(state_mlstm_m[0][:, :, None], state_mlstm_m.shape[1:] + (LANES,))
    state = (state_mlstm_conv[0], state_mlstm_C[0], state_mlstm_n[0], m_lanes, state_hgrn_S[0])
    (y_p, y_s, p_conv, p_c, p_n, p_m, p_s, s_conv, s_c, s_n, s_m, s_s) = _mixer_call(
        x_prompt, x_sample, state, weights)
    return (y_p, y_s, p_conv[None], p_c[None], p_n[None], p_m[None, :, :, 0], p_s[None],
            s_conv[None], s_c[None], s_n[None], s_m[None, :, :, 0], s_s[None])
```

```python
import collections
import functools

import jax
import jax.numpy as jnp
from jax import lax
from jax.experimental import pallas as pl
from jax.experimental.pallas import tpu as pltpu

F32 = jnp.float32
BF16 = jnp.bfloat16

EPS = 1e-6
CONV_W = 4
M_HEADS = 4
H_HEADS = 8
LANES = 128
SUBLANES = 8
MXU_COLS = 256
HIST = SUBLANES
PROMPT_CHUNK = 64
PROMPT_GROUP = 4
HGRN_SUBCHUNK = 32
VMEM_LIMIT_BYTES = 60 * 1024 * 1024
GATE_PIECES_PER_ROUND = (1, 3, 3, 1, 2, 1, 1, 0)

_Group = collections.namedtuple(
    "_Group", "row0 n_seq tc lh convbuf c_in n_in m_in s_in c_out n_out m_out s_out")


def _sigmoid(x):
    return 0.5 * jnp.tanh(0.5 * x) + 0.5


def _silu(x):
    return x * _sigmoid(x)


def _log_sigmoid(x):
    return jnp.minimum(x, 0.0) - jnp.log(1.0 + jnp.exp(-jnp.abs(x)))


def _dot(a, b):
    return jnp.dot(a, b, preferred_element_type=F32)


def _dot_nt(a, b):
    return lax.dot_general(a, b, (((1,), (1,)), ((), ())), preferred_element_type=F32)


def _dot_tn(a, b):
    return lax.dot_general(a, b, (((0,), (0,)), ((), ())), preferred_element_type=F32)


def _transpose_rows(a):
    r = a.shape[0]
    if r < LANES:
        a = jnp.concatenate([a, jnp.zeros((LANES - r, LANES), a.dtype)], axis=0)
    return a.T


def _round_robin(gens):
    gens = list(gens)
    while gens:
        for g in list(gens):
            try:
                next(g)
            except StopIteration:
                gens.remove(g)
        yield


def _mixer_kernel(n_p, tc_p, n_s, tc_s, *refs):
    (xp_ref, xs_ref, conv0_ref, c0_ref, n0_ref, m0_ref, s0_ref,
     g_norm_ref, w_in_ref, conv_w_ref, conv_b_ref, w_q_ref, w_k_ref, w_v_ref, w_gate_ref,
     b_gate_ref, m_ln_ref, m_skip_ref, lb_param_ref, h_norm_ref, w_out_ref, g_final_ref,
     yp_ref, ys_ref, pconv_ref, pc_ref, pn_ref, pm_ref, ps_ref,
     sconv_ref, sc_ref, sn_ref, sm_ref, ss_ref,
     convbuf_p, convbuf_s, xn_s, xc_s, q_s, k_s, v_s, g_s, hm_s, logf_s, kh_s, qs_s, ih_s, oh_s,
     gact_s) = refs

    rows_p = n_p * tc_p
    rows_s = n_s * tc_s
    d_model = xp_ref.shape[-1]
    d_m = convbuf_p.shape[-1]
    dk = w_q_ref.shape[-1]
    dv = w_v_ref.shape[-1]
    hdk = d_m // H_HEADS
    q_scale = dk ** -0.5

    prompt = _Group(0, n_p, tc_p, min(tc_p, HGRN_SUBCHUNK), convbuf_p,
                    pc_ref, pn_ref, pm_ref, ps_ref, pc_ref, pn_ref, pm_ref, ps_ref)
    sample = _Group(rows_p, n_s, tc_s, min(tc_s, HGRN_SUBCHUNK), convbuf_s,
                    c0_ref, n0_ref, m0_ref, s0_ref, sc_ref, sn_ref, sm_ref, ss_ref)
    groups = (prompt, sample)

    t_idx = pl.program_id(1)

    @pl.when(t_idx == 0)
    def _():
        pc_ref[...] = jnp.zeros_like(pc_ref)
        pn_ref[...] = jnp.zeros_like(pn_ref)
        pm_ref[...] = jnp.zeros_like(pm_ref)
        ps_ref[...] = jnp.zeros_like(ps_ref)
        convbuf_p[:, 0:HIST, :] = jnp.zeros((n_p, HIST, d_m), F32)

    @pl.when(t_idx > 0)
    def _():
        convbuf_p[:, 0:HIST, :] = convbuf_p[:, tc_p:tc_p + HIST, :]

    convbuf_s[:, HIST - (CONV_W - 1):HIST, :] = conv0_ref[...]

    x = jnp.concatenate([xp_ref[...].reshape(rows_p, d_model),
                         xs_ref[...].reshape(rows_s, d_model)], axis=0)
    xn = (x * lax.rsqrt(jnp.mean(x * x, -1, keepdims=True) + EPS) * g_norm_ref[...]).astype(BF16)
    xn_s[...] = xn

    def proj(i):
        return _dot(xn, w_in_ref[:, i * d_m:(i + 1) * d_m])

    xm = proj(0)
    for grp, conv_o_ref in ((prompt, pconv_ref), (sample, sconv_ref)):
        n_rows = grp.n_seq * grp.tc
        grp.convbuf[:, HIST:HIST + grp.tc, :] = (
            xm[grp.row0:grp.row0 + n_rows].reshape(grp.n_seq, grp.tc, d_m))
        acc = conv_b_ref[...].reshape(1, 1, d_m)
        for j in range(CONV_W):
            lo = HIST - (CONV_W - 1) + j
            acc = acc + grp.convbuf[:, lo:lo + grp.tc, :] * conv_w_ref[j:j + 1, :].reshape(1, 1, d_m)
        conv_o_ref[...] = grp.convbuf[:, grp.tc + HIST - (CONV_W - 1):grp.tc + HIST, :]
        xc_s[grp.row0:grp.row0 + n_rows, :] = _silu(acc).reshape(n_rows, d_m)

    xm_b = xm.astype(BF16)
    xc_b = xc_s[...].astype(BF16)
    for h in range(M_HEADS):
        xc_h = xc_b[:, h * dv:(h + 1) * dv]
        q_s[:, h * dk:(h + 1) * dk] = _dot(xc_h, w_q_ref[h])
        k_s[:, h * dk:(h + 1) * dk] = _dot(xc_h, w_k_ref[h])
        v_s[:, h * dv:(h + 1) * dv] = _dot(xm_b[:, h * dv:(h + 1) * dv], w_v_ref[h])
    n_qk = M_HEADS * dk
    g_s[...] = (b_gate_ref[...]
                + _dot(q_s[...].astype(BF16), w_gate_ref[0:n_qk, :])
                + _dot(k_s[...].astype(BF16), w_gate_ref[n_qk:2 * n_qk, :])
                + _dot(v_s[...].astype(BF16), w_gate_ref[2 * n_qk:, :]))

    lbp = lb_param_ref[...]
    lbe = jnp.exp(lbp - jnp.max(lbp, axis=0, keepdims=True))
    lb = lbe[0:1, :] / jnp.sum(lbe, axis=0, keepdims=True)
    sg = _sigmoid(proj(3))
    logf_s[...] = jnp.log(lb + (1.0 - lb) * sg)
    kh_s[...] = (1.0 - lb) * (1.0 - sg)
    qs_s[...] = _silu(proj(4))
    ih_s[...] = proj(5)

    neg_inf = F32(-jnp.inf)

    def masks(n):
        row = lax.broadcasted_iota(jnp.int32, (n, n), 0)
        col = lax.broadcasted_iota(jnp.int32, (n, n), 1)
        return col <= row, row <= col

    def mlstm_seq(grp, b, causal, causal_t):
        rows = pl.ds(grp.row0 + b * grp.tc, grp.tc)
        gb = g_s[rows, :]
        gt = _transpose_rows(gb)
        lf_cols = _log_sigmoid(gb)
        lf_rows = _log_sigmoid(gt[M_HEADS:2 * M_HEADS, 0:grp.tc])
        yield
        yield from _round_robin([mlstm_head(grp, b, h, rows, causal, causal_t, gb, gt, lf_cols, lf_rows)
                                 for h in range(M_HEADS)])

    def mlstm_head(grp, b, h, rows, causal, causal_t, gb, gt, lf_cols, lf_rows):
        tc = grp.tc
        i_row = gt[h:h + 1, 0:tc]
        i_col = gb[:, h:h + 1]
        lf_row = lf_rows[h:h + 1, :]
        lf_col = lf_cols[:, M_HEADS + h:M_HEADS + h + 1]
        b_col = jnp.sum(jnp.where(causal, lf_row, 0.0), axis=1, keepdims=True)
        b_row = jnp.sum(jnp.where(causal_t, lf_col, 0.0), axis=0, keepdims=True)
        q = q_s[rows, h * dk:(h + 1) * dk] * q_scale
        k = k_s[rows, h * dk:(h + 1) * dk]
        v_b = v_s[rows, h * dv:(h + 1) * dv].astype(BF16)
        q_b = q.astype(BF16)
        qk = _dot_nt(q_b, k.astype(BF16))
        c_prev = grp.c_in[b, h]
        n_prev = grp.n_in[b, h:h + 1, :]
        qc = _dot(q_b, c_prev.astype(BF16))
        qn = jnp.sum(q * n_prev, axis=1, keepdims=True)
        yield
        m_prev = grp.m_in[b, h:h + 1, 0:1]
        a_row = i_row - b_row
        a_col = i_col - b_col
        log_d = jnp.where(causal, a_row, neg_inf)
        m_col = jnp.maximum(jnp.max(log_d, axis=1, keepdims=True), m_prev)
        yield
        d_mat = jnp.exp(log_d - m_col)
        g_col = jnp.exp(m_prev - m_col)
        s_mat = qk * d_mat
        num = _dot(s_mat.astype(BF16), v_b) + g_col * qc
        den = jnp.sum(s_mat, axis=1, keepdims=True) + g_col * qn
        m_last = m_col[tc - 1:tc, :]
        decay = jnp.exp(m_prev - m_last)
        kw = k * jnp.exp(a_col - m_last)
        grp.c_out[b, h] = decay * grp.c_in[b, h] + _dot_tn(kw.astype(BF16), v_b)
        grp.n_out[b, h:h + 1, :] = decay * n_prev + jnp.sum(kw, axis=0, keepdims=True)
        grp.m_out[b, h:h + 1, :] = jnp.broadcast_to(b_col[tc - 1:tc, :] + m_last, (1, LANES))
        yield
        m_t = b_col + m_col
        hval = num * (1.0 / jnp.maximum(jnp.abs(den), jnp.exp(-m_t)))
        mu = jnp.mean(hval, axis=1, keepdims=True)
        hc = hval - mu
        var = jnp.mean(hc * hc, axis=1, keepdims=True)
        hm_s[rows, h * dv:(h + 1) * dv] = (hc * lax.rsqrt(var + EPS)
                                           * m_ln_ref[:, h * dv:(h + 1) * dv])

    def hgrn_seq(grp, b, causal_h):
        lh = grp.lh
        n_sub = grp.tc // lh
        tril_h = causal_h.astype(F32).astype(BF16)
        rows = [pl.ds(grp.row0 + b * grp.tc + j * lh, lh) for j in range(n_sub)]
        bcs = []
        for j in range(n_sub):
            lf = logf_s[rows[j], :]
            lf_hi = lf.astype(BF16)
            lf_lo = (lf - lf_hi.astype(F32)).astype(BF16)
            bcs.append(_dot(tril_h, lf_hi) + _dot(tril_h, lf_lo))
        yield
        operands = []
        for j, bc in enumerate(bcs):
            b_mid = bc[lh // 2 - 1:lh // 2, :]
            b_last = bc[lh - 1:lh, :]
            qs = qs_s[rows[j], :]
            kh = kh_s[rows[j], :]
            p_all = (qs * jnp.exp(bc - b_mid)).astype(BF16)
            q_all = (kh * jnp.exp(b_mid - bc)).astype(BF16)
            qe_all = (qs * jnp.exp(bc)).astype(BF16)
            kd_all = (kh * jnp.exp(b_last - bc)).astype(BF16)
            e8 = jnp.concatenate([b_last[:, h * hdk:(h + 1) * hdk] for h in range(H_HEADS)], axis=0)
            dec_cols = jnp.exp(_transpose_rows(e8))
            operands.append((rows[j], p_all, q_all, qe_all, kd_all, dec_cols))
        yield
        yield from _round_robin([hgrn_head(grp, b, h, causal_h, operands) for h in range(H_HEADS)])

    def hgrn_head(grp, b, h, causal_h, operands):
        sl = slice(h * hdk, (h + 1) * hdk)
        n_sub = len(operands)
        state = grp.s_in[b, h]
        v_b, pq, inter, o, ms = {}, {}, {}, {}, {}
        for k in range(n_sub + 3):
            if k == 0:
                for j, (rows_j, p_all, q_all, _, _, _) in enumerate(operands):
                    v_b[j] = ih_s[rows_j, sl].astype(BF16)
                    pq[j] = _dot_nt(p_all[:, sl], q_all[:, sl])
                yield
            if k < n_sub:
                _, _, _, qe_all, kd_all, dec_cols = operands[k]
                inter[k] = _dot(qe_all[:, sl], state.astype(BF16))
                state = dec_cols[:, h:h + 1] * state + _dot_tn(kd_all[:, sl], v_b[k])
                if k == n_sub - 1:
                    grp.s_out[b, h] = state
            if 0 <= k - 1 < n_sub:
                a_mat = jnp.where(causal_h, pq[k - 1], 0.0)
                o[k - 1] = _dot(a_mat.astype(BF16), v_b[k - 1]) + inter[k - 1]
            if 0 <= k - 2 < n_sub:
                ms[k - 2] = jnp.mean(o[k - 2] * o[k - 2], axis=1, keepdims=True)
            if 0 <= k - 3 < n_sub:
                oh_s[operands[k - 3][0], sl] = (o[k - 3] * lax.rsqrt(ms[k - 3] + EPS)
                                                * h_norm_ref[:, sl])
            if k < n_sub + 2:
                yield

    gate_acts = ((1, _silu), (2, _sigmoid), (6, _silu))
    pieces = [(slot, sec, act, c) for slot, (sec, act) in enumerate(gate_acts)
              for c in range(d_m // MXU_COLS)]
    assert sum(GATE_PIECES_PER_ROUND) == len(pieces)

    def gate_pieces():
        todo = iter(pieces)
        for n in GATE_PIECES_PER_ROUND:
            for _ in range(n):
                slot, sec, act, c = next(todo)
                col = sec * d_m + c * MXU_COLS
                gact_s[slot, :, c * MXU_COLS:(c + 1) * MXU_COLS] = act(
                    _dot(xn_s[...], w_in_ref[:, col:col + MXU_COLS]))
            yield

    units = [gate_pieces()]
    for grp in groups:
        causal, causal_t = masks(grp.tc)
        units += [mlstm_seq(grp, b, causal, causal_t) for b in range(grp.n_seq)]
    for grp in groups:
        causal_h, _ = masks(grp.lh)
        units += [hgrn_seq(grp, b, causal_h) for b in range(grp.n_seq)]
    for _ in _round_robin(units):
        pass

    hm = hm_s[...] * gact_s[1]
    hm = (hm + m_skip_ref[...] * xc_s[...]) * gact_s[0]
    oh = oh_s[...] * gact_s[2]
    y = (jnp.concatenate([xp_ref[...].reshape(rows_p, d_model),
                          xs_ref[...].reshape(rows_s, d_model)], axis=0)
         + _dot(hm.astype(BF16), w_out_ref[0:d_m, :])
         + _dot(oh.astype(BF16), w_out_ref[d_m:, :]))
    y = y * lax.rsqrt(jnp.mean(y * y, -1, keepdims=True) + EPS) * g_final_ref[...]
    yp_ref[...] = y[0:rows_p].reshape(n_p, tc_p, d_model)
    ys_ref[...] = y[rows_p:].reshape(n_s, tc_s, d_model)


def _const_spec(shape):
    zeros = (0,) * len(shape)
    return pl.BlockSpec(shape, lambda *_: zeros, pipeline_mode=pl.Buffered(1))


def _state_shapes(batch, d_m, dk, dv, hdk):
    return [
        jax.ShapeDtypeStruct((batch, CONV_W - 1, d_m), F32),
        jax.ShapeDtypeStruct((batch, M_HEADS, dk, dv), F32),
        jax.ShapeDtypeStruct((batch, M_HEADS, dk), F32),
        jax.ShapeDtypeStruct((batch, M_HEADS, LANES), F32),
        jax.ShapeDtypeStruct((batch, H_HEADS, hdk, hdk), F32),
    ]


def _state_specs(nb, d_m, dk, dv, hdk, batch_index):
    def spec(shape):
        zeros = (0,) * (len(shape) - 1)
        return pl.BlockSpec(shape, lambda g, t: (batch_index(g, t),) + zeros)
    return [spec((nb, CONV_W - 1, d_m)), spec((nb, M_HEADS, dk, dv)), spec((nb, M_HEADS, dk)),
            spec((nb, M_HEADS, LANES)), spec((nb, H_HEADS, hdk, hdk))]


def _mixer_call(x_p, x_s, state_s, weights):
    b_p, t_p, d_model = x_p.shape
    b_s, tc_s, _ = x_s.shape
    (g_norm, w_in, conv_w, conv_b, w_q, w_k, w_v, w_gate, b_gate, m_ln, m_skip, lb_param,
     h_norm, w_out, g_final) = weights
    d_m = conv_w.shape[-1]
    dk, dv = w_q.shape[-1], w_v.shape[-1]
    hdk = d_m // H_HEADS
    n_p, tc_p = PROMPT_GROUP, PROMPT_CHUNK
    n_groups, n_steps = b_p // n_p, t_p // tc_p
    n_s = b_s // (n_groups * n_steps)
    assert n_groups * n_p == b_p and n_steps * tc_p == t_p and n_s * n_groups * n_steps == b_s
    rows = n_p * tc_p + n_s * tc_s

    xp_spec = pl.BlockSpec((n_p, tc_p, d_model), lambda g, t: (g, t, 0))
    xs_spec = pl.BlockSpec((n_s, tc_s, d_model), lambda g, t: (g * n_steps + t, 0, 0))
    p_state = _state_specs(n_p, d_m, dk, dv, hdk, lambda g, t: g)
    s_state = _state_specs(n_s, d_m, dk, dv, hdk, lambda g, t: g * n_steps + t)

    scratch = [
        pltpu.VMEM((n_p, tc_p + HIST, d_m), F32),
        pltpu.VMEM((n_s, tc_s + HIST, d_m), F32),
        pltpu.VMEM((rows, d_model), BF16),
        pltpu.VMEM((rows, d_m), F32),
        pltpu.VMEM((rows, M_HEADS * dk), F32),
        pltpu.VMEM((rows, M_HEADS * dk), F32),
        pltpu.VMEM((rows, d_m), F32),
        pltpu.VMEM((rows, LANES), F32),
        pltpu.VMEM((rows, d_m), F32),
        pltpu.VMEM((rows, d_m), F32),
        pltpu.VMEM((rows, d_m), F32),
        pltpu.VMEM((rows, d_m), F32),
        pltpu.VMEM((rows, d_m), F32),
        pltpu.VMEM((rows, d_m), F32),
        pltpu.VMEM((3, rows, d_m), F32),
    ]
    return pl.pallas_call(
        functools.partial(_mixer_kernel, n_p, tc_p, n_s, tc_s),
        grid=(n_groups, n_steps),
        in_specs=[xp_spec, xs_spec] + s_state + [_const_spec(w.shape) for w in weights],
        out_specs=[xp_spec, xs_spec] + p_state + s_state,
        out_shape=([jax.ShapeDtypeStruct(x_p.shape, x_p.dtype),
                    jax.ShapeDtypeStruct(x_s.shape, x_s.dtype)]
                   + _state_shapes(b_p, d_m, dk, dv, hdk) + _state_shapes(b_s, d_m, dk, dv, hdk)),
        scratch_shapes=scratch,
        compiler_params=pltpu.CompilerParams(dimension_semantics=("arbitrary", "arbitrary"),
                                             vmem_limit_bytes=VMEM_LIMIT_BYTES),
        name="mixer_step",
    )(x_p, x_s, *state_s, *weights)


def kernel(x_prompt, x_sample, state_mlstm_conv, state_mlstm_C, state_mlstm_n, state_mlstm_m,
           state_hgrn_S, g_norm, w_in, conv_w, conv_b, w_q, w_k, w_v, w_gate, b_gate, m_ln,
           m_skip, lb_param, h_norm, w_out, g_final):
    depth = w_in.shape[0]
    assert depth == 1 and lb_param.shape[0] == 2
    d_m = conv_w.shape[-1]
    n_gate = w_gate.shape[-1]
    weights = (
        g_norm[0][None, :],
        w_in[0].astype(BF16),
        conv_w[0],
        conv_b[0][None, :],
        w_q[0].astype(BF16),
        w_k[0].astype(BF16),
        w_v[0].astype(BF16),
        jnp.pad(w_gate[0], ((0, 0), (0, LANES - n_gate))).astype(BF16),
        jnp.pad(b_gate[0], (0, LANES - n_gate))[None, :],
        m_ln[0].reshape(1, d_m),
        m_skip[0][None, :],
        lb_param,
        h_norm[0].reshape(1, d_m),
        w_out[0].astype(BF16),
        g_final[None, :],
    )
    m_lanes = jnp.broadcast_to(state_mlstm_m[0][:, :, None], state_mlstm_m.shape[1:] + (LANES,))
    state = (state_mlstm_conv[0], state_mlstm_C[0], state_mlstm_n[0], m_lanes, state_hgrn_S[0])
    (y_p, y_s, p_conv, p_c, p_n, p_m, p_s, s_conv, s_c, s_n, s_m, s_s) = _mixer_call(
        x_prompt, x_sample, state, weights)
    return (y_p, y_s, p_conv[None], p_c[None], p_n[None], p_m[None, :, :, 0], p_s[None],
            s_conv[None], s_c[None], s_n[None], s_m[None, :, :, 0], s_s[None])
```

```python
import collections
import functools

import jax
import jax.numpy as jnp
from jax import lax
from jax.experimental import pallas as pl
from jax.experimental.pallas import tpu as pltpu

F32 = jnp.float32
BF16 = jnp.bfloat16

EPS = 1e-6
CONV_W = 4
M_HEADS = 4
H_HEADS = 8
LANES = 128
SUBLANES = 8
MXU_COLS = 256
HIST = SUBLANES
PROMPT_CHUNK = 64
PROMPT_GROUP = 4
HGRN_SUBCHUNK = 32
VMEM_LIMIT_BYTES = 60 * 1024 * 1024
GATE_PIECES_PER_ROUND = (1, 3, 3, 1, 2, 1, 1, 0)

_Group = collections.namedtuple(
    "_Group", "row0 n_seq tc lh convbuf c_in n_in m_in s_in c_out n_out m_out s_out")


def _sigmoid(x):
    return 0.5 * jnp.tanh(0.5 * x) + 0.5


def _silu(x):
    return x * _sigmoid(x)


def _log_sigmoid(x):
    return jnp.minimum(x, 0.0) - jnp.log(1.0 + jnp.exp(-jnp.abs(x)))


def _dot(a, b):
    return jnp.dot(a, b, preferred_element_type=F32)


def _dot_nt(a, b):
    return lax.dot_general(a, b, (((1,), (1,)), ((), ())), preferred_element_type=F32)


def _dot_tn(a, b):
    return lax.dot_general(a, b, (((0,), (0,)), ((), ())), preferred_element_type=F32)


def _transpose_rows(a):
    r = a.shape[0]
    if r < LANES:
        a = jnp.concatenate([a, jnp.zeros((LANES - r, LANES), a.dtype)], axis=0)
    return a.T


def _round_robin(gens):
    gens = list(gens)
    while gens:
        for g in list(gens):
            try:
                next(g)
            except StopIteration:
                gens.remove(g)
        yield


def _mixer_kernel(n_p, tc_p, n_s, tc_s, *refs):
    (xp_ref, xs_ref, conv0_ref, c0_ref, n0_ref, m0_ref, s0_ref,
     g_norm_ref, w_in_ref, conv_w_ref, conv_b_ref, w_q_ref, w_k_ref, w_v_ref, w_gate_ref,
     b_gate_ref, m_ln_ref, m_skip_ref, lb_param_ref, h_norm_ref, w_out_ref, g_final_ref,
     yp_ref, ys_ref, pconv_ref, pc_ref, pn_ref, pm_ref, ps_ref,
     sconv_ref, sc_ref, sn_ref, sm_ref, ss_ref,
     convbuf_p, convbuf_s, xn_s, xc_s, q_s, k_s, v_s, g_s, hm_s, logf_s, kh_s, qs_s, ih_s, oh_s,
     gact_s) = refs

    rows_p = n_p * tc_p
    rows_s = n_s * tc_s
    d_model = xp_ref.shape[-1]
    d_m = convbuf_p.shape[-1]
    dk = w_q_ref.shape[-1]
    dv = w_v_ref.shape[-1]
    hdk = d_m // H_HEADS
    q_scale = dk ** -0.5

    prompt = _Group(0, n_p, tc_p, min(tc_p, HGRN_SUBCHUNK), convbuf_p,
                    pc_ref, pn_ref, pm_ref, ps_ref, pc_ref, pn_ref, pm_ref, ps_ref)
    sample = _Group(rows_p, n_s, tc_s, min(tc_s, HGRN_SUBCHUNK), convbuf_s,
                    c0_ref, n0_ref, m0_ref, s0_ref, sc_ref, sn_ref, sm_ref, ss_ref)
    groups = (prompt, sample)

    t_idx = pl.program_id(1)

    @pl.when(t_idx == 0)
    def _():
        pc_ref[...] = jnp.zeros_like(pc_ref)
        pn_ref[...] = jnp.zeros_like(pn_ref)
        pm_ref[...] = jnp.zeros_like(pm_ref)
        ps_ref[...] = jnp.zeros_like(ps_ref)
        convbuf_p[:, 0:HIST, :] = jnp.zeros((n_p, HIST, d_m), F32)

    @pl.when(t_idx > 0)
    def _():
        convbuf_p[:, 0:HIST, :] = convbuf_p[:, tc_p:tc_p + HIST, :]

    convbuf_s[:, HIST - (CONV_W - 1):HIST, :] = conv0_ref[...]

    x = jnp.concatenate([xp_ref[...].reshape(rows_p, d_model),
                         xs_ref[...].reshape(rows_s, d_model)], axis=0)
    xn = (x * lax.rsqrt(jnp.mean(x * x, -1, keepdims=True) + EPS) * g_norm_ref[...]).astype(BF16)
    xn_s[...] = xn

    def proj(i):
        return _dot(xn, w_in_ref[:, i * d_m:(i + 1) * d_m])

    xm = proj(0)
    for grp, conv_o_ref in ((prompt, pconv_ref), (sample, sconv_ref)):
        n_rows = grp.n_seq * grp.tc
        grp.convbuf[:, HIST:HIST + grp.tc, :] = (
            xm[grp.row0:grp.row0 + n_rows].reshape(grp.n_seq, grp.tc, d_m))
        acc = conv_b_ref[...].reshape(1, 1, d_m)
        for j in range(CONV_W):
            lo = HIST - (CONV_W - 1) + j
            acc = acc + grp.convbuf[:, lo:lo + grp.tc, :] * conv_w_ref[j:j + 1, :].reshape(1, 1, d_m)
        conv_o_ref[...] = grp.convbuf[:, grp.tc + HIST - (CONV_W - 1):grp.tc + HIST, :]
        xc_s[grp.row0:grp.row0 + n_rows, :] = _silu(acc).reshape(n_rows, d_m)

    xm_b = xm.astype(BF16)
    xc_b = xc_s[...].astype(BF16)
    for h in range(M_HEADS):
        xc_h = xc_b[:, h * dv:(h + 1) * dv]
        q_s[:, h * dk:(h + 1) * dk] = _dot(xc_h, w_q_ref[h])
        k_s[:, h * dk:(h + 1) * dk] = _dot(xc_h, w_k_ref[h])
        v_s[:, h * dv:(h + 1) * dv] = _dot(xm_b[:, h * dv:(h + 1) * dv], w_v_ref[h])
    n_qk = M_HEADS * dk
    g_s[...] = (b_gate_ref[...]
                + _dot(q_s[...].astype(BF16), w_gate_ref[0:n_qk, :])
                + _dot(k_s[...].astype(BF16), w_gate_ref[n_qk:2 * n_qk, :])
                + _dot(v_s[...].astype(BF16), w_gate_ref[2 * n_qk:, :]))

    lbp = lb_param_ref[...]
    lbe = jnp.exp(lbp - jnp.max(lbp, axis=0, keepdims=True))
    lb = lbe[0:1, :] / jnp.sum(lbe, axis=0, keepdims=True)
    sg = _sigmoid(proj(3))
    logf_s[...] = jnp.log(lb + (1.0 - lb) * sg)
    kh_s[...] = (1.0 - lb) * (1.0 - sg)
    qs_s[...] = _silu(proj(4))
    ih_s[...] = proj(5)

    neg_inf = F32(-jnp.inf)

    def masks(n):
        row = lax.broadcasted_iota(jnp.int32, (n, n), 0)
        col = lax.broadcasted_iota(jnp.int32, (n, n), 1)
        return col <= row, row <= col

    def mlstm_seq(grp, b, causal, causal_t):
        rows = pl.ds(grp.row0 + b * grp.tc, grp.tc)
        gb = g_s[rows, :]
        gt = _transpose_rows(gb)
        lf_cols = _log_sigmoid(gb)
        lf_rows = _log_sigmoid(gt[M_HEADS:2 * M_HEADS, 0:grp.tc])
        qks = [_dot_nt((q_s[rows, h * dk:(h + 1) * dk] * q_scale).astype(BF16),
                       k_s[rows, h * dk:(h + 1) * dk].astype(BF16)) for h in range(M_HEADS)]
        yield
        yield from _round_robin([mlstm_head(grp, b, h, rows, causal, causal_t, gb, gt, lf_cols, lf_rows,
                                            qks[h]) for h in range(M_HEADS)])

    def mlstm_head(grp, b, h, rows, causal, causal_t, gb, gt, lf_cols, lf_rows, qk):
        tc = grp.tc
        i_row = gt[h:h + 1, 0:tc]
        i_col = gb[:, h:h + 1]
        lf_row = lf_rows[h:h + 1, :]
        lf_col = lf_cols[:, M_HEADS + h:M_HEADS + h + 1]
        b_col = jnp.sum(jnp.where(causal, lf_row, 0.0), axis=1, keepdims=True)
        b_row = jnp.sum(jnp.where(causal_t, lf_col, 0.0), axis=0, keepdims=True)
        q = q_s[rows, h * dk:(h + 1) * dk] * q_scale
        k = k_s[rows, h * dk:(h + 1) * dk]
        v_b = v_s[rows, h * dv:(h + 1) * dv].astype(BF16)
        q_b = q.astype(BF16)
        c_prev = grp.c_in[b, h]
        n_prev = grp.n_in[b, h:h + 1, :]
        qc = _dot(q_b, c_prev.astype(BF16))
        qn = jnp.sum(q * n_prev, axis=1, keepdims=True)
        yield
        m_prev = grp.m_in[b, h:h + 1, 0:1]
        a_row = i_row - b_row
        a_col = i_col - b_col
        log_d = jnp.where(causal, a_row, neg_inf)
        m_col = jnp.maximum(jnp.max(log_d, axis=1, keepdims=True), m_prev)
        yield
        d_mat = jnp.exp(log_d - m_col)
        g_col = jnp.exp(m_prev - m_col)
        s_mat = qk * d_mat
        num = _dot(s_mat.astype(BF16), v_b) + g_col * qc
        den = jnp.sum(s_mat, axis=1, keepdims=True) + g_col * qn
        m_last = m_col[tc - 1:tc, :]
        decay = jnp.exp(m_prev - m_last)
        kw = k * jnp.exp(a_col - m_last)
        grp.c_out[b, h] = decay * grp.c_in[b, h] + _dot_tn(kw.astype(BF16), v_b)
        grp.n_out[b, h:h + 1, :] = decay * n_prev + jnp.sum(kw, axis=0, keepdims=True)
        grp.m_out[b, h:h + 1, :] = jnp.broadcast_to(b_col[tc - 1:tc, :] + m_last, (1, LANES))
        yield
        m_t = b_col + m_col
        hval = num * (1.0 / jnp.maximum(jnp.abs(den), jnp.exp(-m_t)))
        mu = jnp.mean(hval, axis=1, keepdims=True)
        hc = hval - mu
        var = jnp.mean(hc * hc, axis=1, keepdims=True)
        hm_s[rows, h * dv:(h + 1) * dv] = (hc * lax.rsqrt(var + EPS)
                                           * m_ln_ref[:, h * dv:(h + 1) * dv])

    def hgrn_seq(grp, b, causal_h):
        lh = grp.lh
        n_sub = grp.tc // lh
        tril_h = causal_h.astype(F32).astype(BF16)
        rows = [pl.ds(grp.row0 + b * grp.tc + j * lh, lh) for j in range(n_sub)]
        bcs = []
        for j in range(n_sub):
            lf = logf_s[rows[j], :]
            lf_hi = lf.astype(BF16)
            lf_lo = (lf - lf_hi.astype(F32)).astype(BF16)
            bcs.append(_dot(tril_h, lf_hi) + _dot(tril_h, lf_lo))
        yield
        operands = []
        for j, bc in enumerate(bcs):
            b_mid = bc[lh // 2 - 1:lh // 2, :]
            b_last = bc[lh - 1:lh, :]
            qs = qs_s[rows[j], :]
            kh = kh_s[rows[j], :]
            p_all = (qs * jnp.exp(bc - b_mid)).astype(BF16)
            q_all = (kh * jnp.exp(b_mid - bc)).astype(BF16)
            qe_all = (qs * jnp.exp(bc)).astype(BF16)
            kd_all = (kh * jnp.exp(b_last - bc)).astype(BF16)
            e8 = jnp.concatenate([b_last[:, h * hdk:(h + 1) * hdk] for h in range(H_HEADS)], axis=0)
            dec_cols = jnp.exp(_transpose_rows(e8))
            operands.append((rows[j], p_all, q_all, qe_all, kd_all, dec_cols))
        yield
        yield from _round_robin([hgrn_head(grp, b, h, causal_h, operands) for h in range(H_HEADS)])

    def hgrn_head(grp, b, h, causal_h, operands):
        sl = slice(h * hdk, (h + 1) * hdk)
        n_sub = len(operands)
        state = grp.s_in[b, h]
        v_b, pq, inter, o, ms = {}, {}, {}, {}, {}
        for k in range(n_sub + 3):
            if k == 0:
                for j, (rows_j, p_all, q_all, _, _, _) in enumerate(operands):
                    v_b[j] = ih_s[rows_j, sl].astype(BF16)
                    pq[j] = _dot_nt(p_all[:, sl], q_all[:, sl])
                yield
            if k < n_sub:
                _, _, _, qe_all, kd_all, dec_cols = operands[k]
                inter[k] = _dot(qe_all[:, sl], state.astype(BF16))
                state = dec_cols[:, h:h + 1] * state + _dot_tn(kd_all[:, sl], v_b[k])
                if k == n_sub - 1:
                    grp.s_out[b, h] = state
            if 0 <= k - 1 < n_sub:
                a_mat = jnp.where(causal_h, pq[k - 1], 0.0)
                o[k - 1] = _dot(a_mat.astype(BF16), v_b[k - 1]) + inter[k - 1]
            if 0 <= k - 2 < n_sub:
                ms[k - 2] = jnp.mean(o[k - 2] * o[k - 2], axis=1, keepdims=True)
            if 0 <= k - 3 < n_sub:
                oh_s[operands[k - 3][0], sl] = (o[k - 3] * lax.rsqrt(ms[k - 3] + EPS)
                                                * h_norm_ref[:, sl])
            if k < n_sub + 2:
                yield

    gate_acts = ((1, _silu), (2, _sigmoid), (6, _silu))
    pieces = [(slot, sec, act, c) for slot, (sec, act) in enumerate(gate_acts)
              for c in range(d_m // MXU_COLS)]
    assert sum(GATE_PIECES_PER_ROUND) == len(pieces)

    def gate_pieces():
        todo = iter(pieces)
        for n in GATE_PIECES_PER_ROUND:
            for _ in range(n):
                slot, sec, act, c = next(todo)
                col = sec * d_m + c * MXU_COLS
                gact_s[slot, :, c * MXU_COLS:(c + 1) * MXU_COLS] = act(
                    _dot(xn_s[...], w_in_ref[:, col:col + MXU_COLS]))
            yield

    units = [gate_pieces()]
    for grp in groups:
        causal, causal_t = masks(grp.tc)
        units += [mlstm_seq(grp, b, causal, causal_t) for b in range(grp.n_seq)]
    for grp in groups:
        causal_h, _ = masks(grp.lh)
        units += [hgrn_seq(grp, b, causal_h) for b in range(grp.n_seq)]
    for _ in _round_robin(units):
        pass

    hm = hm_s[...] * gact_s[1]
    hm = (hm + m_skip_ref[...] * xc_s[...]) * gact_s[0]
    oh = oh_s[...] * gact_s[2]
    y = (jnp.concatenate([xp_ref[...].reshape(rows_p, d_model),
                          xs_ref[...].reshape(rows_s, d_model)], axis=0)
         + _dot(hm.astype(BF16), w_out_ref[0:d_m, :])
         + _dot(oh.astype(BF16), w_out_ref[d_m:, :]))
    y = y * lax.rsqrt(jnp.mean(y * y, -1, keepdims=True) + EPS) * g_final_ref[...]
    yp_ref[...] = y[0:rows_p].reshape(n_p, tc_p, d_model)
    ys_ref[...] = y[rows_p:].reshape(n_s, tc_s, d_model)


def _const_spec(shape):
    zeros = (0,) * len(shape)
    return pl.BlockSpec(shape, lambda *_: zeros, pipeline_mode=pl.Buffered(1))


def _state_shapes(batch, d_m, dk, dv, hdk):
    return [
        jax.ShapeDtypeStruct((batch, CONV_W - 1, d_m), F32),
        jax.ShapeDtypeStruct((batch, M_HEADS, dk, dv), F32),
        jax.ShapeDtypeStruct((batch, M_HEADS, dk), F32),
        jax.ShapeDtypeStruct((batch, M_HEADS, LANES), F32),
        jax.ShapeDtypeStruct((batch, H_HEADS, hdk, hdk), F32),
    ]


def _state_specs(nb, d_m, dk, dv, hdk, batch_index):
    def spec(shape):
        zeros = (0,) * (len(shape) - 1)
        return pl.BlockSpec(shape, lambda g, t: (batch_index(g, t),) + zeros)
    return [spec((nb, CONV_W - 1, d_m)), spec((nb, M_HEADS, dk, dv)), spec((nb, M_HEADS, dk)),
            spec((nb, M_HEADS, LANES)), spec((nb, H_HEADS, hdk, hdk))]


def _mixer_call(x_p, x_s, state_s, weights):
    b_p, t_p, d_model = x_p.shape
    b_s, tc_s, _ = x_s.shape
    (g_norm, w_in, conv_w, conv_b, w_q, w_k, w_v, w_gate, b_gate, m_ln, m_skip, lb_param,
     h_norm, w_out, g_final) = weights
    d_m = conv_w.shape[-1]
    dk, dv = w_q.shape[-1], w_v.shape[-1]
    hdk = d_m // H_HEADS
    n_p, tc_p = PROMPT_GROUP, PROMPT_CHUNK
    n_groups, n_steps = b_p // n_p, t_p // tc_p
    n_s = b_s // (n_groups * n_steps)
    assert n_groups * n_p == b_p and n_steps * tc_p == t_p and n_s * n_groups * n_steps == b_s
    rows = n_p * tc_p + n_s * tc_s

    xp_spec = pl.BlockSpec((n_p, tc_p, d_model), lambda g, t: (g, t, 0))
    xs_spec = pl.BlockSpec((n_s, tc_s, d_model), lambda g, t: (g * n_steps + t, 0, 0))
    p_state = _state_specs(n_p, d_m, dk, dv, hdk, lambda g, t: g)
    s_state = _state_specs(n_s, d_m, dk, dv, hdk, lambda g, t: g * n_steps + t)

    scratch = [
        pltpu.VMEM((n_p, tc_p + HIST, d_m), F32),
        pltpu.VMEM((n_s, tc_s + HIST, d_m), F32),
        pltpu.VMEM((rows, d_model), BF16),
        pltpu.VMEM((rows, d_m), F32),
        pltpu.VMEM((rows, M_HEADS * dk), F32),
        pltpu.VMEM((rows, M_HEADS * dk), F32),
        pltpu.VMEM((rows, d_m), F32),
        pltpu.VMEM((rows, LANES), F32),
        pltpu.VMEM((rows, d_m), F32),
        pltpu.VMEM((rows, d_m), F32),
        pltpu.VMEM((rows, d_m), F32),
        pltpu.VMEM((rows, d_m), F32),
        pltpu.VMEM((rows, d_m), F32),
        pltpu.VMEM((rows, d_m), F32),
        pltpu.VMEM((3, rows, d_m), F32),
    ]
    return pl.pallas_call(
        functools.partial(_mixer_kernel, n_p, tc_p, n_s, tc_s),
        grid=(n_groups, n_steps),
        in_specs=[xp_spec, xs_spec] + s_state + [_const_spec(w.shape) for w in weights],
        out_specs=[xp_spec, xs_spec] + p_state + s_state,
        out_shape=([jax.ShapeDtypeStruct(x_p.shape, x_p.dtype),
                    jax.ShapeDtypeStruct(x_s.shape, x_s.dtype)]
                   + _state_shapes(b_p, d_m, dk, dv, hdk) + _state_shapes(b_s, d_m, dk, dv, hdk)),
        scratch_shapes=scratch,
        compiler_params=pltpu.CompilerParams(dimension_semantics=("arbitrary", "arbitrary"),
                                             vmem_limit_bytes=VMEM_LIMIT_BYTES),
        name="mixer_step",
    )(x_p, x_s, *state_s, *weights)


def kernel(x_prompt, x_sample, state_mlstm_conv, state_mlstm_C, state_mlstm_n, state_mlstm_m,
           state_hgrn_S, g_norm, w_in, conv_w, conv_b, w_q, w_k, w_v, w_gate, b_gate, m_ln,
           m_skip, lb_param, h_norm, w_out, g_final):
    depth = w_in.shape[0]
    assert depth == 1 and lb_param.shape[0] == 2
    d_m = conv_w.shape[-1]
    n_gate = w_gate.shape[-1]
    weights = (
        g_norm[0][None, :],
        w_in[0].astype(BF16),
        conv_w[0],
        conv_b[0][None, :],
        w_q[0].astype(BF16),
        w_k[0].astype(BF16),
        w_v[0].astype(BF16),
        jnp.pad(w_gate[0], ((0, 0), (0, LANES - n_gate))).astype(BF16),
        jnp.pad(b_gate[0], (0, LANES - n_gate))[None, :],
        m_ln[0].reshape(1, d_m),
        m_skip[0][None, :],
        lb_param,
        h_norm[0].reshape(1, d_m),
        w_out[0].astype(BF16),
        g_final[None, :],
    )
    m_lanes = jnp.broadcast_to(state_mlstm_m[0][:, :, None], state_mlstm_m.shape[1:] + (LANES,))
    state = (state_mlstm_conv[0], state_mlstm_C[0], state_mlstm_n[0], m_lanes, state_hgrn_S[0])
    (y_p, y_s, p_conv, p_c, p_n, p_m, p_s, s_conv, s_c, s_n, s_m, s_s) = _mixer_call(
        x_prompt, x_sample, state, weights)
    return (y_p, y_s, p_conv[None], p_c[None], p_n[None], p_m[None, :, :, 0], p_s[None],
            s_conv[None], s_c[None], s_n[None], s_m[None, :, :, 0], s_s[None])
```
